```python
import jax, jax.numpy as jnp
from jax import lax
import numpy as np

D_MODEL = 1024
BATCH = 8
SEQ = 8192
DEPTH = 1

GRID_W = 64
PLE_DIM = 256
ATTN_HEADS = 8
ATTN_KV_HEADS = 2
ATTN_HEAD_DIM = 64
ATTN_GROUP = ATTN_HEADS // ATTN_KV_HEADS
Q_BLOCK = 128
ROPE_THETA = 10000.0
ATTN_Q_W = ATTN_HEADS * ATTN_HEAD_DIM
ATTN_KV_W = ATTN_KV_HEADS * ATTN_HEAD_DIM
DN_HEADS = 4
DN_HEAD_K = 128
DN_HEAD_V = 128
DN_CHUNK = 64
CONV_W = 5
DN_QK_W = DN_HEADS * DN_HEAD_K
DN_V_W = DN_HEADS * DN_HEAD_V
DN_QKV_W = 2 * DN_QK_W + DN_V_W
IN_W = ATTN_Q_W + 2 * ATTN_KV_W + DN_QKV_W + DN_V_W + 4 * DN_HEADS
MIX_W = ATTN_Q_W + DN_V_W
FF_DIM = -(-8 * D_MODEL // (3 * 256)) * 256
EPS = 1e-6

kernel_name = "hymba_gqa_axialrope_gdeltanet_swiglu_ple"


def rms_norm(x, gain):
    xf = x.astype(jnp.float32)
    y = xf * lax.rsqrt(jnp.mean(xf * xf, axis=-1, keepdims=True) + EPS)
    return (y * gain.astype(jnp.float32)).astype(x.dtype)


def l2_norm(x):
    return x * lax.rsqrt(jnp.sum(x * x, axis=-1, keepdims=True) + EPS)


def rope_1d(x, ang):
    c = jnp.cos(ang)[:, None, :].astype(x.dtype)
    s = jnp.sin(ang)[:, None, :].astype(x.dtype)
    half = x.shape[-1] // 2
    x1, x2 = x[..., :half], x[..., half:]
    return jnp.concatenate([x1 * c - x2 * s, x2 * c + x1 * s], axis=-1)


def axial_rope(x, ang_row, ang_col):
    half = x.shape[-1] // 2
    return jnp.concatenate([rope_1d(x[..., :half], ang_row), rope_1d(x[..., half:], ang_col)], axis=-1)


def grid_angles(s):
    rows = s // GRID_W
    row = jnp.broadcast_to(jnp.arange(rows, dtype=jnp.float32)[:, None], (rows, GRID_W)).reshape(s)
    col = jnp.broadcast_to(jnp.arange(GRID_W, dtype=jnp.float32)[None, :], (rows, GRID_W)).reshape(s)
    sec = ATTN_HEAD_DIM // 2
    inv_freq = ROPE_THETA ** (-jnp.arange(0, sec, 2, dtype=jnp.float32) / sec)
    return row[:, None] * inv_freq[None, :], col[:, None] * inv_freq[None, :]


def block_attention(q, k, v):
    b, s, _, d = q.shape
    nb = s // Q_BLOCK
    qb = jnp.moveaxis(q.reshape(b, nb, Q_BLOCK, ATTN_KV_HEADS, ATTN_GROUP, d), 1, 0)
    scale = d ** -0.5

    def one_block(q_blk):
        scores = jnp.einsum('bqkgd,bskd->bkgqs', q_blk, k, preferred_element_type=jnp.float32) * scale
        probs = jax.nn.softmax(scores, axis=-1).astype(v.dtype)
        return jnp.einsum('bkgqs,bskd->bqkgd', probs, v)

    out = lax.map(one_block, qb)
    return jnp.moveaxis(out, 0, 1).reshape(b, s, ATTN_HEADS * d)


def chunked_gated_delta_rule(q, k, v, g, beta):
    b, h, s, dk = q.shape
    dv = v.shape[-1]
    c = DN_CHUNK
    n = s // c
    q = q.reshape(b, h, n, c, dk)
    k = k.reshape(b, h, n, c, dk)
    v = v.reshape(b, h, n, c, dv)
    beta = beta.reshape(b, h, n, c)
    G = jnp.cumsum(g.reshape(b, h, n, c), axis=-1)
    incl = jnp.tril(jnp.ones((c, c), dtype=bool))
    strict = jnp.tril(jnp.ones((c, c), dtype=bool), -1)
    diff = G[..., :, None] - G[..., None, :]
    decay = jnp.where(incl, jnp.exp(jnp.where(incl, diff, 0.0)), 0.0)
    k_beta = k * beta[..., None]
    a_mat = jnp.where(strict, jnp.einsum('bhnid,bhnjd->bhnij', k_beta, k) * decay, 0.0) + jnp.eye(c, dtype=q.dtype)
    rhs = jnp.concatenate([v * beta[..., None], k_beta * jnp.exp(G)[..., None]], axis=-1)
    sol = lax.linalg.triangular_solve(a_mat, rhs, left_side=True, lower=True, unit_diagonal=True)
    u, w = sol[..., :dv], sol[..., dv:]
    qk = jnp.einsum('bhnid,bhnjd->bhnij', q, k) * decay
    q_dec = q * jnp.exp(G)[..., None]
    k_dec = k * jnp.exp(G[..., -1:] - G)[..., None]
    g_tot = jnp.exp(G[..., -1])
    xs = tuple(jnp.moveaxis(t, 2, 0) for t in (w, u, qk, q_dec, k_dec, g_tot))

    def step(state, inp):
        w_c, u_c, qk_c, qd_c, kd_c, gt_c = inp
        v_new = u_c - jnp.einsum('bhck,bhkv->bhcv', w_c, state)
        o_c = jnp.einsum('bhck,bhkv->bhcv', qd_c, state) + jnp.einsum('bhcj,bhjv->bhcv', qk_c, v_new)
        state = state * gt_c[..., None, None] + jnp.einsum('bhck,bhcv->bhkv', kd_c, v_new)
        return state, o_c

    state0 = jnp.zeros((b, h, dk, dv), dtype=q.dtype)
    _, o = lax.scan(step, state0, xs)
    return jnp.moveaxis(o, 0, 2).reshape(b, h, s, dv)


def deltanet_mixer(qkv, z, b_raw, a_raw, conv_w, a_log, dt_bias, norm_gain):
    b, s, _ = qkv.shape
    dtype = qkv.dtype
    f32 = jnp.float32
    qkv = jax.nn.silu(lax.conv_general_dilated(
        qkv, conv_w[:, None, :], window_strides=(1,), padding=[(CONV_W // 2, CONV_W // 2)],
        dimension_numbers=('NWC', 'WIO', 'NWC'), feature_group_count=DN_QKV_W))
    q, k, v = jnp.split(qkv.astype(f32), [DN_QK_W, 2 * DN_QK_W], axis=-1)
    q = l2_norm(q.reshape(b, s, DN_HEADS, DN_HEAD_K).transpose(0, 2, 1, 3)) * (DN_HEAD_K ** -0.5)
    k = l2_norm(k.reshape(b, s, DN_HEADS, DN_HEAD_K).transpose(0, 2, 1, 3))
    v = v.reshape(b, s, DN_HEADS, DN_HEAD_V).transpose(0, 2, 1, 3)
    beta = jax.nn.sigmoid(b_raw.astype(f32))
    g = -jnp.exp(a_log.astype(f32)) * jax.nn.softplus(a_raw.astype(f32) + dt_bias.astype(f32))
    q2 = jnp.concatenate([q, jnp.flip(q, axis=2)], axis=1)
    k2 = jnp.concatenate([k, jnp.flip(k, axis=2)], axis=1)
    v2 = jnp.concatenate([v, jnp.flip(v, axis=2)], axis=1)
    g2 = jnp.concatenate([g[..., :DN_HEADS], jnp.flip(g[..., DN_HEADS:], axis=1)], axis=-1).transpose(0, 2, 1)
    beta2 = jnp.concatenate([beta[..., :DN_HEADS], jnp.flip(beta[..., DN_HEADS:], axis=1)], axis=-1).transpose(0, 2, 1)
    o2 = chunked_gated_delta_rule(q2, k2, v2, g2, beta2)
    o = o2[:, :DN_HEADS] + jnp.flip(o2[:, DN_HEADS:], axis=2)
    o = o.transpose(0, 2, 1, 3)
    o = rms_norm(o, norm_gain) * jax.nn.silu(z.reshape(b, s, DN_HEADS, DN_HEAD_V).astype(f32))
    return o.reshape(b, s, DN_V_W).astype(dtype)


def setup_inputs(seed: int = 0) -> dict:
    key = jax.random.key(seed)
    ks = jax.random.split(key, 20)
    nrm = lambda k_, shape, scale: jax.random.normal(k_, shape, dtype=jnp.float32) * scale
    gain = lambda k_, shape: 1.0 + 0.01 * jax.random.normal(k_, shape, dtype=jnp.float32)
    dt = jnp.exp(jax.random.uniform(ks[8], (DEPTH, 2 * DN_HEADS), minval=np.log(0.001), maxval=np.log(0.1)))
    return {
        "x": nrm(ks[0], (BATCH, SEQ, D_MODEL), 1.0),
        "p": nrm(ks[1], (DEPTH, BATCH, SEQ, PLE_DIM), 1.0),
        "norm_mix": gain(ks[2], (DEPTH, D_MODEL)),
        "w_in": nrm(ks[3], (DEPTH, D_MODEL, IN_W), D_MODEL ** -0.5),
        "conv_w": nrm(ks[4], (DEPTH, CONV_W, DN_QKV_W), CONV_W ** -0.5),
        "q_norm": gain(ks[5], (DEPTH, ATTN_HEAD_DIM)),
        "k_norm": gain(ks[6], (DEPTH, ATTN_HEAD_DIM)),
        "a_log": jnp.log(jax.random.uniform(ks[7], (DEPTH, 2 * DN_HEADS), minval=1.0, maxval=16.0)),
        "dt_bias": dt + jnp.log(-jnp.expm1(-dt)),
        "dn_norm": gain(ks[9], (DEPTH, DN_HEAD_V)),
        "w_out": nrm(ks[10], (DEPTH, MIX_W, D_MODEL), MIX_W ** -0.5),
        "norm_ffn": gain(ks[11], (DEPTH, D_MODEL)),
        "w_gate": nrm(ks[12], (DEPTH, D_MODEL, FF_DIM), D_MODEL ** -0.5),
        "w_up": nrm(ks[13], (DEPTH, D_MODEL, FF_DIM), D_MODEL ** -0.5),
        "w_down": nrm(ks[14], (DEPTH, FF_DIM, D_MODEL), FF_DIM ** -0.5),
        "norm_ple": gain(ks[15], (DEPTH, D_MODEL)),
        "w_ple_gate": nrm(ks[16], (DEPTH, D_MODEL, D_MODEL), D_MODEL ** -0.5),
        "w_ple": nrm(ks[17], (DEPTH, PLE_DIM, D_MODEL), PLE_DIM ** -0.5),
        "norm_final": gain(ks[18], (D_MODEL,)),
    }


def reference(x, p, norm_mix, w_in, conv_w, q_norm, k_norm, a_log, dt_bias, dn_norm, w_out,
              norm_ffn, w_gate, w_up, w_down, norm_ple, w_ple_gate, w_ple, norm_final):
    b, s, _ = x.shape
    ang_row, ang_col = grid_angles(s)
    split_at = list(np.cumsum([ATTN_Q_W, ATTN_KV_W, ATTN_KV_W, DN_QKV_W, DN_V_W, 2 * DN_HEADS]))
    h = x
    for i in range(DEPTH):
        hn = rms_norm(h, norm_mix[i])
        proj = hn @ w_in[i]
        aq, ak, av, dqkv, dz, dbeta, dalpha = jnp.split(proj, split_at, axis=-1)
        aq = axial_rope(rms_norm(aq.reshape(b, s, ATTN_HEADS, ATTN_HEAD_DIM), q_norm[i]), ang_row, ang_col)
        ak = axial_rope(rms_norm(ak.reshape(b, s, ATTN_KV_HEADS, ATTN_HEAD_DIM), k_norm[i]), ang_row, ang_col)
        av = av.reshape(b, s, ATTN_KV_HEADS, ATTN_HEAD_DIM)
        attn_out = block_attention(aq, ak, av)
        dn_out = deltanet_mixer(dqkv, dz, dbeta, dalpha, conv_w[i], a_log[i], dt_bias[i], dn_norm[i])
        mixed = jnp.concatenate([attn_out.astype(h.dtype), dn_out.astype(h.dtype)], axis=-1)
        h = h + mixed @ w_out[i]
        hn = rms_norm(h, norm_ffn[i])
        h = h + (jax.nn.silu(hn @ w_gate[i]) * (hn @ w_up[i])) @ w_down[i]
        gate = jax.nn.sigmoid(rms_norm(h, norm_ple[i]) @ w_ple_gate[i])
        h = h + gate * (p[i] @ w_ple[i])
    return rms_norm(h, norm_final)
```

```python
import functools

import numpy as np
import jax
import jax.numpy as jnp
from jax import lax
from jax.experimental import pallas as pl
from jax.experimental.pallas import tpu as pltpu

F32 = jnp.float32
BF16 = jnp.bfloat16

LANES = 128
SUBLANES = 8
VMEM_LIMIT = 60000 * 1024

GRID_W = 64
ATTN_HEADS = 8
ATTN_KV_HEADS = 2
ATTN_HEAD_DIM = 64
ATTN_GROUP = ATTN_HEADS // ATTN_KV_HEADS
ROPE_THETA = 10000.0
DN_HEADS = 4
DN_HEAD = 128
DN_CHUNK = 64
CONV_W = 5
EPS = 1e-6

Q_EXP_W = ATTN_HEADS * LANES
KV_W = ATTN_KV_HEADS * ATTN_HEAD_DIM
DN_QKV_W = 3 * DN_HEADS * DN_HEAD
DN_V_W = DN_HEADS * DN_HEAD
ATTN_Q_W = ATTN_HEADS * ATTN_HEAD_DIM

ROW_TILE = 512
Q_TILE = 512
KV_TILE = 1024
FF_CHUNK = 256
DN_BLOCK = 256
CONV_HALO = SUBLANES


def _dot(a, b):
    return jnp.dot(a, b, preferred_element_type=F32)


def _dot_nt(a, b):
    return lax.dot_general(a, b, (((1,), (1,)), ((), ())), preferred_element_type=F32)


def _rms(x, gain):
    return x * lax.rsqrt(jnp.mean(x * x, axis=-1, keepdims=True) + EPS) * gain


def _sigmoid(x):
    return 1.0 / (1.0 + jnp.exp(-x))


def _inproj_kernel(x_ref, gain_ref, w_ref, cos_ref, sin_ref, qg_ref, kg_ref,
                   q_ref, k_ref, v_ref, dqkv_ref, dz_ref, ba_ref):
    tm = x_ref.shape[0]
    hb = _rms(x_ref[...], gain_ref[...]).astype(BF16)
    cos = cos_ref[...]
    sin = sin_ref[...]
    lane = lax.broadcasted_iota(jnp.int32, (tm, LANES), 1)
    low16 = (lane % 32) < 16

    def rope(t):
        partner = jnp.where(low16, pltpu.roll(t, LANES - 16, 1), pltpu.roll(t, 16, 1))
        return t * cos + partner * sin

    c0 = 0
    q_all = _dot(hb, w_ref[:, c0:c0 + Q_EXP_W])
    for h in range(ATTN_HEADS):
        qh = q_all[:, h * LANES:(h + 1) * LANES]
        ss = jnp.sum(qh * qh, axis=-1, keepdims=True) * (1.0 / ATTN_HEAD_DIM)
        qn = qh * lax.rsqrt(ss + EPS) * qg_ref[...]
        q_ref[:, h * LANES:(h + 1) * LANES] = (rope(qn) * (ATTN_HEAD_DIM ** -0.5)).astype(BF16)
    c0 += Q_EXP_W

    kv = _dot(hb, w_ref[:, c0:c0 + 2 * KV_W])
    kk = kv[:, :KV_W]
    head0 = lane < ATTN_HEAD_DIM
    k2 = kk * kk
    ss0 = jnp.sum(jnp.where(head0, k2, 0.0), axis=-1, keepdims=True) * (1.0 / ATTN_HEAD_DIM)
    ss1 = jnp.sum(jnp.where(head0, 0.0, k2), axis=-1, keepdims=True) * (1.0 / ATTN_HEAD_DIM)
    rs = jnp.where(head0, lax.rsqrt(ss0 + EPS), lax.rsqrt(ss1 + EPS))
    k_ref[...] = rope(kk * rs * kg_ref[...]).astype(BF16)
    v_ref[...] = kv[:, KV_W:].astype(BF16)
    c0 += 2 * KV_W

    dqkv_ref[...] = _dot(hb, w_ref[:, c0:c0 + DN_QKV_W]).astype(BF16)
    c0 += DN_QKV_W
    zb = _dot(hb, w_ref[:, c0:c0 + DN_V_W + LANES])
    dz_ref[...] = zb[:, :DN_V_W].astype(BF16)
    ba_ref[...] = zb[:, DN_V_W:]


def _inproj(xf, gain, w_cat, cos_t, sin_t, qg, kg, seq, tm):
    t, d = xf.shape
    nseq = seq // tm
    wcols = w_cat.shape[1]
    row = lambda i: (i, 0)
    const = lambda i: (0, 0)
    tab = lambda i: (i % nseq, 0)
    return pl.pallas_call(
        _inproj_kernel,
        grid=(t // tm,),
        in_specs=[
            pl.BlockSpec((tm, d), row),
            pl.BlockSpec((1, d), const),
            pl.BlockSpec((d, wcols), const),
            pl.BlockSpec((tm, LANES), tab),
            pl.BlockSpec((tm, LANES), tab),
            pl.BlockSpec((1, LANES), const),
            pl.BlockSpec((1, LANES), const),
        ],
        out_specs=[
            pl.BlockSpec((tm, Q_EXP_W), row),
            pl.BlockSpec((tm, KV_W), row),
            pl.BlockSpec((tm, KV_W), row),
            pl.BlockSpec((tm, DN_QKV_W), row),
            pl.BlockSpec((tm, DN_V_W), row),
            pl.BlockSpec((tm, LANES), row),
        ],
        out_shape=[
            jax.ShapeDtypeStruct((t, Q_EXP_W), BF16),
            jax.ShapeDtypeStruct((t, KV_W), BF16),
            jax.ShapeDtypeStruct((t, KV_W), BF16),
            jax.ShapeDtypeStruct((t, DN_QKV_W), BF16),
            jax.ShapeDtypeStruct((t, DN_V_W), BF16),
            jax.ShapeDtypeStruct((t, LANES), F32),
        ],
        compiler_params=pltpu.CompilerParams(
            dimension_semantics=("parallel",), vmem_limit_bytes=VMEM_LIMIT),
        name="inproj",
    )(xf, gain, w_cat, cos_t, sin_t, qg, kg)


def _attn_kernel(q_ref, k_ref, v_ref, o_ref, m_ref, l_ref, acc_ref):
    kv_step = pl.program_id(2)
    tq = q_ref.shape[1]

    @pl.when(kv_step == 0)
    def _():
        m_ref[...] = jnp.full(m_ref.shape, -jnp.inf, F32)
        l_ref[...] = jnp.zeros(l_ref.shape, F32)
        acc_ref[...] = jnp.zeros(acc_ref.shape, F32)

    k = k_ref[0]
    v = v_ref[0]
    for h in range(ATTN_HEADS):
        q = q_ref[0, :, h * LANES:(h + 1) * LANES]
        s = _dot_nt(q, k)
        m_prev = m_ref[h]
        m_new = jnp.maximum(m_prev, jnp.max(s, axis=-1, keepdims=True))
        alpha = jnp.exp(m_prev - m_new)
        p = jnp.exp(s - m_new[:, :1])
        l_ref[h] = alpha * l_ref[h] + jnp.sum(p, axis=-1, keepdims=True)
        acc_ref[h] = alpha * acc_ref[h] + _dot(p.astype(BF16), v)
        m_ref[h] = m_new

    @pl.when(kv_step == pl.num_programs(2) - 1)
    def _():
        lane = lax.broadcasted_iota(jnp.int32, (tq, LANES), 1)
        lower = lane < ATTN_HEAD_DIM
        for j in range(ATTN_HEADS // 2):
            a0 = acc_ref[2 * j] / l_ref[2 * j]
            a1 = acc_ref[2 * j + 1] / l_ref[2 * j + 1]
            if (2 * j) // ATTN_GROUP == 0:
                out = jnp.where(lower, a0, pltpu.roll(a1, ATTN_HEAD_DIM, 1))
            else:
                out = jnp.where(lower, pltpu.roll(a0, ATTN_HEAD_DIM, 1), a1)
            o_ref[0, :, j * LANES:(j + 1) * LANES] = out.astype(BF16)


def _attention(q, k, v, tq, tk):
    b, s, _ = q.shape
    return pl.pallas_call(
        _attn_kernel,
        grid=(b, s // tq, s // tk),
        in_specs=[
            pl.BlockSpec((1, tq, Q_EXP_W), lambda bi, qi, ki: (bi, qi, 0)),
            pl.BlockSpec((1, tk, KV_W), lambda bi, qi, ki: (bi, ki, 0)),
            pl.BlockSpec((1, tk, KV_W), lambda bi, qi, ki: (bi, ki, 0)),
        ],
        out_specs=pl.BlockSpec((1, tq, ATTN_Q_W), lambda bi, qi, ki: (bi, qi, 0)),
        out_shape=jax.ShapeDtypeStruct((b, s, ATTN_Q_W), BF16),
        scratch_shapes=[
            pltpu.VMEM((ATTN_HEADS, tq, LANES), F32),
            pltpu.VMEM((ATTN_HEADS, tq, LANES), F32),
            pltpu.VMEM((ATTN_HEADS, tq, LANES), F32),
        ],
        compiler_params=pltpu.CompilerParams(
            dimension_semantics=("parallel", "parallel", "arbitrary"), vmem_limit_bytes=VMEM_LIMIT),
        name="attention",
    )(q, k, v)


def _split3(x):
    x1 = x.astype(BF16)
    r1 = x - x1.astype(F32)
    x2 = r1.astype(BF16)
    x3 = (r1 - x2.astype(F32)).astype(BF16)
    return x1, x2, x3


def _dn_kernel(rev, qkv_ref, prev_ref, next_ref, ba_ref, convw_ref, alog_ref, dtb_ref,
               o_ref, s_ref):
    n = DN_BLOCK
    step = pl.program_id(1)
    nblk = pl.num_programs(1)
    blk = (nblk - 1 - step) if rev else step

    @pl.when(step == 0)
    def _():
        s_ref[...] = jnp.zeros(s_ref.shape, F32)

    top = jnp.where(blk == 0, 0.0, prev_ref[0].astype(F32))
    bot = jnp.where(blk == nblk - 1, 0.0, next_ref[0].astype(F32))
    ext = jnp.concatenate([top, qkv_ref[0].astype(F32), bot], axis=0)
    rows = n + 2 * CONV_HALO
    acc = None
    for j in range(CONV_W):
        shift = (CONV_W // 2 - j) % rows
        xs = ext if shift == 0 else pltpu.roll(ext, shift, 0)
        term = xs[CONV_HALO:CONV_HALO + n] * convw_ref[j:j + 1, :]
        acc = term if acc is None else acc + term
    y = acc * _sigmoid(acc)

    ba = ba_ref[0]
    beta_all = _sigmoid(ba)
    sp_arg = ba + dtb_ref[...]
    softplus = jnp.maximum(sp_arg, 0.0) + jnp.log(1.0 + jnp.exp(-jnp.abs(sp_arg)))
    g_all = -jnp.exp(alog_ref[...]) * softplus

    row = lax.broadcasted_iota(jnp.int32, (n, n), 0)
    col = lax.broadcasted_iota(jnp.int32, (n, n), 1)
    same = (row // DN_CHUNK) == (col // DN_CHUNK)
    if rev:
        incl = same & (row <= col)
        strict = same & (row < col)
    else:
        incl = same & (row >= col)
        strict = same & (row > col)
    incl_bf = jnp.where(incl, 1.0, 0.0).astype(BF16)
    eye = jnp.where(row == col, 1.0, 0.0)

    def level_mask(lb):
        rb = row >> lb
        cb = col >> lb
        if rev:
            return (cb == rb + 1) & ((cb & 1) == 1)
        return (rb == cb + 1) & ((rb & 1) == 1)

    order = range(n // DN_CHUNK - 1, -1, -1) if rev else range(n // DN_CHUNK)
    dir_off = DN_HEADS if rev else 0
    for h in range(DN_HEADS):
        q = y[:, h * DN_HEAD:(h + 1) * DN_HEAD]
        k = y[:, (DN_HEADS + h) * DN_HEAD:(DN_HEADS + h + 1) * DN_HEAD]
        v = y[:, (2 * DN_HEADS + h) * DN_HEAD:(2 * DN_HEADS + h + 1) * DN_HEAD]
        q = q * lax.rsqrt(jnp.sum(q * q, axis=-1, keepdims=True) + EPS) * (DN_HEAD ** -0.5)
        k = k * lax.rsqrt(jnp.sum(k * k, axis=-1, keepdims=True) + EPS)
        hd = dir_off + h
        beta = jnp.broadcast_to(beta_all[:, hd:hd + 1], (n, LANES))
        g = jnp.broadcast_to(g_all[:, 2 * DN_HEADS + hd:2 * DN_HEADS + hd + 1], (n, LANES))

        gc = None
        for part in _split3(g):
            t = _dot(incl_bf, part)
            gc = t if gc is None else gc + t
        g_row = jnp.transpose(gc)[0:1, :]
        diff = jnp.concatenate([gc, gc], axis=1) - g_row
        decay = jnp.where(incl, jnp.exp(jnp.where(incl, diff, 0.0)), 0.0)

        kb = k * beta
        k_bf = k.astype(BF16)
        a_mat = jnp.where(strict, _dot_nt(kb.astype(BF16), k_bf) * decay, 0.0)
        x_inv = eye - jnp.where(level_mask(0), a_mat, 0.0)
        for lb in range(1, 6):
            e_b = jnp.where(level_mask(lb), a_mat, 0.0).astype(BF16)
            x_bf = x_inv.astype(BF16)
            x_inv = x_inv - _dot(_dot(x_bf, e_b).astype(BF16), x_bf)

        e_gc = jnp.exp(gc)
        rhs = jnp.concatenate([v * beta, kb * e_gc], axis=1).astype(BF16)
        sol = _dot(x_inv.astype(BF16), rhs)
        u = sol[:, :DN_HEAD]
        w = sol[:, DN_HEAD:].astype(BF16)
        qk = jnp.where(incl, _dot_nt(q.astype(BF16), k_bf) * decay, 0.0).astype(BF16)
        qd = (q * e_gc).astype(BF16)

        gl_rows = []
        for c in range(n // DN_CHUNK):
            r = c * DN_CHUNK + (0 if rev else DN_CHUNK - 1)
            gl_rows.append(gc[r:r + 1, :])
        gl = jnp.concatenate(
            [jnp.broadcast_to(t, (DN_CHUNK, LANES)) for t in gl_rows], axis=0)
        kd_t = jnp.transpose(k * jnp.exp(gl - gc)).astype(BF16)

        state = s_ref[h]
        o_parts = [None] * (n // DN_CHUNK)
        for c in order:
            lo, hi = c * DN_CHUNK, (c + 1) * DN_CHUNK
            s_bf = state.astype(BF16)
            vn = u[lo:hi] - _dot(w[lo:hi], s_bf)
            pieces = []
            if lo:
                pieces.append(jnp.zeros((lo, LANES), F32))
            pieces.append(vn)
            if n - hi:
                pieces.append(jnp.zeros((n - hi, LANES), F32))
            vn_pad = jnp.concatenate(pieces, axis=0).astype(BF16)
            o_parts[c] = _dot(qd[lo:hi], s_bf) + _dot(qk[lo:hi], vn_pad)
            state = state * jnp.exp(gl_rows[c]) + _dot(kd_t, vn_pad)
        s_ref[h] = state
        o_ref[0, :, h * DN_HEAD:(h + 1) * DN_HEAD] = jnp.concatenate(o_parts, axis=0)


def _deltanet(dqkv, ba, conv_w, alog_pad, dtb_pad, rev):
    b, s, _ = dqkv.shape
    nblk = s // DN_BLOCK
    per = DN_BLOCK // CONV_HALO
    nhalo = s // CONV_HALO

    def pos(c):
        return (nblk - 1 - c) if rev else c

    return pl.pallas_call(
        functools.partial(_dn_kernel, rev),
        grid=(b, nblk),
        in_specs=[
            pl.BlockSpec((1, DN_BLOCK, DN_QKV_W), lambda bi, c: (bi, pos(c), 0)),
            pl.BlockSpec((1, CONV_HALO, DN_QKV_W),
                         lambda bi, c: (bi, jnp.maximum(pos(c) * per - 1, 0), 0)),
            pl.BlockSpec((1, CONV_HALO, DN_QKV_W),
                         lambda bi, c: (bi, jnp.minimum((pos(c) + 1) * per, nhalo - 1), 0)),
            pl.BlockSpec((1, DN_BLOCK, LANES), lambda bi, c: (bi, pos(c), 0)),
            pl.BlockSpec((CONV_W, DN_QKV_W), lambda bi, c: (0, 0)),
            pl.BlockSpec((1, LANES), lambda bi, c: (0, 0)),
            pl.BlockSpec((1, LANES), lambda bi, c: (0, 0)),
        ],
        out_specs=pl.BlockSpec((1, DN_BLOCK, DN_V_W), lambda bi, c: (bi, pos(c), 0)),
        out_shape=jax.ShapeDtypeStruct((b, s, DN_V_W), F32),
        scratch_shapes=[pltpu.VMEM((DN_HEADS, DN_HEAD, DN_HEAD), F32)],
        compiler_params=pltpu.CompilerParams(
            dimension_semantics=("parallel", "arbitrary"), vmem_limit_bytes=VMEM_LIMIT),
        name="deltanet_bwd" if rev else "deltanet_fwd",
    )(dqkv, dqkv, dqkv, ba, conv_w, alog_pad, dtb_pad)


def _post_kernel(ff_chunk, final_norm, x_ref, attn_ref, of_ref, ob_ref, dz_ref, p_ref, dng_ref, wout_ref,
                 nffn_ref, wg_ref, wu_ref, wd_ref, nple_ref, wpg_ref, wple_ref, nfin_ref, out_ref):
    o = of_ref[...] + ob_ref[...]
    z = dz_ref[...].astype(F32)
    parts = [attn_ref[...]]
    for h in range(DN_HEADS):
        sl = slice(h * DN_HEAD, (h + 1) * DN_HEAD)
        zh = z[:, sl]
        parts.append((_rms(o[:, sl], dng_ref[...]) * (zh * _sigmoid(zh))).astype(BF16))
    mixed = jnp.concatenate(parts, axis=1)
    h1 = x_ref[...] + _dot(mixed, wout_ref[...])

    hn = _rms(h1, nffn_ref[...]).astype(BF16)
    h2 = h1
    ff = wg_ref.shape[1]
    for c in range(ff // ff_chunk):
        sl = slice(c * ff_chunk, (c + 1) * ff_chunk)
        gate = _dot(hn, wg_ref[:, sl])
        up = _dot(hn, wu_ref[:, sl])
        act = (gate * _sigmoid(gate) * up).astype(BF16)
        h2 = h2 + _dot(act, wd_ref[sl, :])

    hp = _rms(h2, nple_ref[...]).astype(BF16)
    gate = _sigmoid(_dot(hp, wpg_ref[...]))
    h3 = h2 + gate * _dot(p_ref[...].astype(BF16), wple_ref[...])
    out_ref[...] = _rms(h3, nfin_ref[...]) if final_norm else h3


def _post(xf, attn, o_f, o_b, dz, pf, dng, w_out, nffn, w_gate, w_up, w_down, nple, w_pg, w_ple,
          nfin, tm, ff_chunk, final_norm):
    t, d = xf.shape
    row = lambda i: (i, 0)
    const = lambda i: (0, 0)

    def resident(shape):
        return pl.BlockSpec(shape, const, pipeline_mode=pl.Buffered(1))

    ff = w_gate.shape[1]
    return pl.pallas_call(
        functools.partial(_post_kernel, ff_chunk, final_norm),
        grid=(t // tm,),
        in_specs=[
            pl.BlockSpec((tm, d), row),
            pl.BlockSpec((tm, ATTN_Q_W), row),
            pl.BlockSpec((tm, DN_V_W), row),
            pl.BlockSpec((tm, DN_V_W), row),
            pl.BlockSpec((tm, DN_V_W), row),
            pl.BlockSpec((tm, pf.shape[1]), row),
            resident((1, DN_HEAD)),
            resident((ATTN_Q_W + DN_V_W, d)),
            resident((1, d)),
            resident((d, ff)),
            resident((d, ff)),
            resident((ff, d)),
            resident((1, d)),
            resident((d, d)),
            resident((pf.shape[1], d)),
            resident((1, d)),
        ],
        out_specs=pl.BlockSpec((tm, d), row),
        out_shape=jax.ShapeDtypeStruct((t, d), F32),
        compiler_params=pltpu.CompilerParams(
            dimension_semantics=("parallel",), vmem_limit_bytes=VMEM_LIMIT),
        name="post",
    )(xf, attn, o_f, o_b, dz, pf, dng, w_out, nffn, w_gate, w_up, w_down, nple, w_pg, w_ple, nfin)


def _rope_tables(seq):
    pos = jnp.arange(seq, dtype=jnp.int32)
    rowcol = jnp.stack([(pos // GRID_W).astype(F32), (pos % GRID_W).astype(F32)], axis=1)
    sec = ATTN_HEAD_DIM // 2
    inv_freq = ROPE_THETA ** (-jnp.arange(0, sec, 2, dtype=F32) / sec)
    d = np.arange(LANES) % ATTN_HEAD_DIM
    axis = d // sec
    idx = d % (sec // 2)
    sign = np.where((d % sec) < sec // 2, -1.0, 1.0).astype(np.float32)
    ang = rowcol[:, axis] * inv_freq[idx][None, :]
    return jnp.cos(ang), jnp.sin(ang) * sign[None, :]


def kernel(x, p, norm_mix, w_in, conv_w, q_norm, k_norm, a_log, dt_bias, dn_norm, w_out,
           norm_ffn, w_gate, w_up, w_down, norm_ple, w_ple_gate, w_ple, norm_final):
    b, s, d = x.shape
    depth = w_in.shape[0]
    t = b * s
    tm, tq, tk = min(ROW_TILE, s), min(Q_TILE, s), min(KV_TILE, s)
    cos_t, sin_t = _rope_tables(s)

    h = x.reshape(t, d)
    for i in range(depth):
        wi = w_in[i].astype(BF16)
        c = 0
        wq = wi[:, c:c + ATTN_Q_W]; c += ATTN_Q_W
        wk = wi[:, c:c + KV_W]; c += KV_W
        wv = wi[:, c:c + KV_W]; c += KV_W
        wdn = wi[:, c:c + DN_QKV_W]; c += DN_QKV_W
        wz = wi[:, c:c + DN_V_W]; c += DN_V_W
        wba = wi[:, c:]
        wq = wq.reshape(d, ATTN_HEADS, ATTN_HEAD_DIM)
        zeros = jnp.zeros_like(wq)
        lower = (np.arange(ATTN_HEADS) // ATTN_GROUP == 0)[None, :, None]
        wq_exp = jnp.concatenate(
            [jnp.where(lower, wq, zeros), jnp.where(lower, zeros, wq)], axis=-1).reshape(d, Q_EXP_W)
        wba = jnp.pad(wba, ((0, 0), (0, LANES - wba.shape[1])))
        w_cat = jnp.concatenate([wq_exp, wk, wv, wdn, wz, wba], axis=1)

        qg = jnp.tile(q_norm[i], LANES // ATTN_HEAD_DIM)[None, :]
        kg = jnp.tile(k_norm[i], LANES // ATTN_HEAD_DIM)[None, :]
        q, k, v, dqkv, dz, ba = _inproj(h, norm_mix[i][None, :], w_cat, cos_t, sin_t, qg, kg, s, tm)

        attn = _attention(q.reshape(b, s, Q_EXP_W), k.reshape(b, s, KV_W), v.reshape(b, s, KV_W),
                          tq, tk)

        pad = (2 * DN_HEADS, LANES - 4 * DN_HEADS)
        alog_pad = jnp.pad(a_log[i], pad)[None, :]
        dtb_pad = jnp.pad(dt_bias[i], pad)[None, :]
        dqkv3 = dqkv.reshape(b, s, DN_QKV_W)
        ba3 = ba.reshape(b, s, LANES)
        o_f = _deltanet(dqkv3, ba3, conv_w[i], alog_pad, dtb_pad, False)
        o_b = _deltanet(dqkv3, ba3, conv_w[i], alog_pad, dtb_pad, True)

        h = _post(h, attn.reshape(t, ATTN_Q_W), o_f.reshape(t, DN_V_W), o_b.reshape(t, DN_V_W), dz,
                  p[i].reshape(t, -1), dn_norm[i][None, :], w_out[i].astype(BF16),
                  norm_ffn[i][None, :], w_gate[i].astype(BF16), w_up[i].astype(BF16),
                  w_down[i].astype(BF16), norm_ple[i][None, :], w_ple_gate[i].astype(BF16),
                  w_ple[i].astype(BF16),
                  norm_final[None, :], tm, FF_CHUNK, i == depth - 1)
    return h.reshape(b, s, d)
```

```python
import functools

import numpy as np
import jax
import jax.numpy as jnp
from jax import lax
from jax.experimental import pallas as pl
from jax.experimental.pallas import tpu as pltpu

F32 = jnp.float32
BF16 = jnp.bfloat16

LANES = 128
SUBLANES = 8
VMEM_LIMIT = 60000 * 1024

GRID_W = 64
ATTN_HEADS = 8
ATTN_KV_HEADS = 2
ATTN_HEAD_DIM = 64
ATTN_GROUP = ATTN_HEADS // ATTN_KV_HEADS
ROPE_THETA = 10000.0
DN_HEADS = 4
DN_HEAD = 128
DN_CHUNK = 64
CONV_W = 5
EPS = 1e-6

Q_EXP_W = ATTN_HEADS * LANES
KV_W = ATTN_KV_HEADS * ATTN_HEAD_DIM
DN_QKV_W = 3 * DN_HEADS * DN_HEAD
DN_V_W = DN_HEADS * DN_HEAD
ATTN_Q_W = ATTN_HEADS * ATTN_HEAD_DIM

ROW_TILE = 512
Q_TILE = 256
KV_TILE = 1024
Q_SCALE = ATTN_HEAD_DIM ** -0.5 * float(np.log2(np.e))
MAX_UNSHIFTED_SCORE = 40.0
FF_CHUNK = 256
DN_BLOCK = 256
CONV_HALO = SUBLANES


def _dot(a, b):
    return jnp.dot(a, b, preferred_element_type=F32)


def _dot_nt(a, b):
    return lax.dot_general(a, b, (((1,), (1,)), ((), ())), preferred_element_type=F32)


def _rms(x, gain):
    return x * lax.rsqrt(jnp.mean(x * x, axis=-1, keepdims=True) + EPS) * gain


def _sigmoid(x):
    return 1.0 / (1.0 + jnp.exp(-x))


def _inproj_kernel(x_ref, gain_ref, w_ref, cos_ref, sin_ref, qg_ref, kg_ref,
                   q_ref, k_ref, va_ref, vb_ref, dqkv_ref, dz_ref, ba_ref):
    tm = x_ref.shape[0]
    hb = _rms(x_ref[...], gain_ref[...]).astype(BF16)
    cos = cos_ref[...]
    sin = sin_ref[...]
    lane = lax.broadcasted_iota(jnp.int32, (tm, LANES), 1)
    low16 = (lane % 32) < 16

    def rope(t):
        partner = jnp.where(low16, pltpu.roll(t, LANES - 16, 1), pltpu.roll(t, 16, 1))
        return t * cos + partner * sin

    c0 = 0
    q_all = _dot(hb, w_ref[:, c0:c0 + Q_EXP_W])
    for h in range(ATTN_HEADS):
        qh = q_all[:, h * LANES:(h + 1) * LANES]
        ss = jnp.sum(qh * qh, axis=-1, keepdims=True) * (1.0 / ATTN_HEAD_DIM)
        qn = qh * lax.rsqrt(ss + EPS) * qg_ref[...]
        q_ref[:, h * LANES:(h + 1) * LANES] = (rope(qn) * Q_SCALE).astype(BF16)
    c0 += Q_EXP_W

    kv = _dot(hb, w_ref[:, c0:c0 + 2 * KV_W])
    kk = kv[:, :KV_W]
    head0 = lane < ATTN_HEAD_DIM
    k2 = kk * kk
    ss0 = jnp.sum(jnp.where(head0, k2, 0.0), axis=-1, keepdims=True) * (1.0 / ATTN_HEAD_DIM)
    ss1 = jnp.sum(jnp.where(head0, 0.0, k2), axis=-1, keepdims=True) * (1.0 / ATTN_HEAD_DIM)
    rs = jnp.where(head0, lax.rsqrt(ss0 + EPS), lax.rsqrt(ss1 + EPS))
    k_ref[...] = rope(kk * rs * kg_ref[...]).astype(BF16)
    vv = kv[:, KV_W:]
    va_ref[...] = jnp.where(head0, vv, 1.0).astype(BF16)
    vb_ref[...] = jnp.where(head0, 1.0, vv).astype(BF16)
    c0 += 2 * KV_W

    dqkv_ref[...] = _dot(hb, w_ref[:, c0:c0 + DN_QKV_W]).astype(BF16)
    c0 += DN_QKV_W
    zb = _dot(hb, w_ref[:, c0:c0 + DN_V_W + LANES])
    dz_ref[...] = zb[:, :DN_V_W].astype(BF16)
    ba_ref[...] = zb[:, DN_V_W:]


def _inproj(xf, gain, w_cat, cos_t, sin_t, qg, kg, seq, tm):
    t, d = xf.shape
    nseq = seq // tm
    wcols = w_cat.shape[1]
    row = lambda i: (i, 0)
    const = lambda i: (0, 0)
    tab = lambda i: (i % nseq, 0)
    return pl.pallas_call(
        _inproj_kernel,
        grid=(t // tm,),
        in_specs=[
            pl.BlockSpec((tm, d), row),
            pl.BlockSpec((1, d), const),
            pl.BlockSpec((d, wcols), const),
            pl.BlockSpec((tm, LANES), tab),
            pl.BlockSpec((tm, LANES), tab),
            pl.BlockSpec((1, LANES), const),
            pl.BlockSpec((1, LANES), const),
        ],
        out_specs=[
            pl.BlockSpec((tm, Q_EXP_W), row),
            pl.BlockSpec((tm, KV_W), row),
            pl.BlockSpec((tm, KV_W), row),
            pl.BlockSpec((tm, KV_W), row),
            pl.BlockSpec((tm, DN_QKV_W), row),
            pl.BlockSpec((tm, DN_V_W), row),
            pl.BlockSpec((tm, LANES), row),
        ],
        out_shape=[
            jax.ShapeDtypeStruct((t, Q_EXP_W), BF16),
            jax.ShapeDtypeStruct((t, KV_W), BF16),
            jax.ShapeDtypeStruct((t, KV_W), BF16),
            jax.ShapeDtypeStruct((t, KV_W), BF16),
            jax.ShapeDtypeStruct((t, DN_QKV_W), BF16),
            jax.ShapeDtypeStruct((t, DN_V_W), BF16),
            jax.ShapeDtypeStruct((t, LANES), F32),
        ],
        compiler_params=pltpu.CompilerParams(
            dimension_semantics=("parallel",), vmem_limit_bytes=VMEM_LIMIT),
        name="inproj",
    )(xf, gain, w_cat, cos_t, sin_t, qg, kg)


def _attn_kernel(tk, bounded_ref, q_ref, k_ref, va_ref, vb_ref, o_ref, qs_ref, acc_ref, m_ref):
    tq = q_ref.shape[1]
    nkv = k_ref.shape[1] // tk
    for g in range(ATTN_KV_HEADS):
        for j in range(ATTN_GROUP):
            h = g * ATTN_GROUP + j
            qs_ref[g, j * tq:(j + 1) * tq, :] = q_ref[0, :, h * LANES:(h + 1) * LANES]
    acc_ref[...] = jnp.zeros(acc_ref.shape, F32)

    def tiles(i):
        off = pl.multiple_of(i * tk, tk)
        kt = k_ref[0, pl.ds(off, tk), :]
        return kt, (va_ref[0, pl.ds(off, tk), :], vb_ref[0, pl.ds(off, tk), :])

    @pl.when(bounded_ref[0] == 1)
    def _():
        def body(i, carry):
            kt, vt = tiles(i)
            for g in range(ATTN_KV_HEADS):
                p = jnp.exp2(_dot_nt(qs_ref[g], kt)).astype(BF16)
                acc_ref[g] += _dot(p, vt[g])
            return carry

        lax.fori_loop(0, nkv, body, 0)

    @pl.when(bounded_ref[0] == 0)
    def _():
        m_ref[...] = jnp.full(m_ref.shape, -jnp.inf, F32)

        def body(i, carry):
            kt, vt = tiles(i)
            for g in range(ATTN_KV_HEADS):
                s = _dot_nt(qs_ref[g], kt)
                m_prev = m_ref[g]
                m_new = jnp.maximum(m_prev, jnp.max(s, axis=-1, keepdims=True))
                p = jnp.exp2(s - m_new[:, :1]).astype(BF16)
                acc_ref[g] = jnp.exp2(m_prev - m_new) * acc_ref[g] + _dot(p, vt[g])
                m_ref[g] = m_new
            return carry

        lax.fori_loop(0, nkv, body, 0)

    lane = lax.broadcasted_iota(jnp.int32, (tq, LANES), 1)
    lower = lane < ATTN_HEAD_DIM
    for g in range(ATTN_KV_HEADS):
        for jp in range(ATTN_GROUP // 2):
            halves = []
            for j in (2 * jp, 2 * jp + 1):
                a = acc_ref[g, j * tq:(j + 1) * tq, :]
                halves.append(a / pltpu.roll(a, ATTN_HEAD_DIM, 1))
            if g == 0:
                out = jnp.where(lower, halves[0], pltpu.roll(halves[1], ATTN_HEAD_DIM, 1))
            else:
                out = jnp.where(lower, pltpu.roll(halves[0], ATTN_HEAD_DIM, 1), halves[1])
            c = g * (ATTN_GROUP // 2) + jp
            o_ref[0, :, c * LANES:(c + 1) * LANES] = out.astype(BF16)


def _attention(bounded, q, k, va, vb, tq, tk):
    b, s, _ = q.shape
    kv_spec = pl.BlockSpec((1, s, KV_W), lambda bi, qi, flag: (bi, 0, 0))
    return pl.pallas_call(
        functools.partial(_attn_kernel, tk),
        grid_spec=pltpu.PrefetchScalarGridSpec(
            num_scalar_prefetch=1,
            grid=(b, s // tq),
            in_specs=[
                pl.BlockSpec((1, tq, Q_EXP_W), lambda bi, qi, flag: (bi, qi, 0)),
                kv_spec, kv_spec, kv_spec,
            ],
            out_specs=pl.BlockSpec((1, tq, ATTN_Q_W), lambda bi, qi, flag: (bi, qi, 0)),
            scratch_shapes=[
                pltpu.VMEM((ATTN_KV_HEADS, ATTN_GROUP * tq, LANES), BF16),
                pltpu.VMEM((ATTN_KV_HEADS, ATTN_GROUP * tq, LANES), F32),
                pltpu.VMEM((ATTN_KV_HEADS, ATTN_GROUP * tq, LANES), F32),
            ],
        ),
        out_shape=jax.ShapeDtypeStruct((b, s, ATTN_Q_W), BF16),
        compiler_params=pltpu.CompilerParams(
            dimension_semantics=("parallel", "arbitrary"), vmem_limit_bytes=VMEM_LIMIT),
        name="attention",
    )(bounded, q, k, va, vb)


def _split3(x):
    x1 = x.astype(BF16)
    r1 = x - x1.astype(F32)
    x2 = r1.astype(BF16)
    x3 = (r1 - x2.astype(F32)).astype(BF16)
    return x1, x2, x3


def _dn_kernel(rev, qkv_ref, prev_ref, next_ref, ba_ref, convw_ref, alog_ref, dtb_ref,
               o_ref, s_ref):
    n = DN_BLOCK
    step = pl.program_id(1)
    nblk = pl.num_programs(1)
    blk = (nblk - 1 - step) if rev else step

    @pl.when(step == 0)
    def _():
        s_ref[...] = jnp.zeros(s_ref.shape, F32)

    top = jnp.where(blk == 0, 0.0, prev_ref[0].astype(F32))
    bot = jnp.where(blk == nblk - 1, 0.0, next_ref[0].astype(F32))
    ext = jnp.concatenate([top, qkv_ref[0].astype(F32), bot], axis=0)
    rows = n + 2 * CONV_HALO
    acc = None
    for j in range(CONV_W):
        shift = (CONV_W // 2 - j) % rows
        xs = ext if shift == 0 else pltpu.roll(ext, shift, 0)
        term = xs[CONV_HALO:CONV_HALO + n] * convw_ref[j:j + 1, :]
        acc = term if acc is None else acc + term
    y = acc * _sigmoid(acc)

    ba = ba_ref[0]
    beta_all = _sigmoid(ba)
    sp_arg = ba + dtb_ref[...]
    softplus = jnp.maximum(sp_arg, 0.0) + jnp.log(1.0 + jnp.exp(-jnp.abs(sp_arg)))
    g_all = -jnp.exp(alog_ref[...]) * softplus

    row = lax.broadcasted_iota(jnp.int32, (n, n), 0)
    col = lax.broadcasted_iota(jnp.int32, (n, n), 1)
    same = (row // DN_CHUNK) == (col // DN_CHUNK)
    if rev:
        incl = same & (row <= col)
        strict = same & (row < col)
    else:
        incl = same & (row >= col)
        strict = same & (row > col)
    incl_bf = jnp.where(incl, 1.0, 0.0).astype(BF16)
    eye = jnp.where(row == col, 1.0, 0.0)

    def level_mask(lb):
        rb = row >> lb
        cb = col >> lb
        if rev:
            return (cb == rb + 1) & ((cb & 1) == 1)
        return (rb == cb + 1) & ((rb & 1) == 1)

    order = range(n // DN_CHUNK - 1, -1, -1) if rev else range(n // DN_CHUNK)
    dir_off = DN_HEADS if rev else 0
    for h in range(DN_HEADS):
        q = y[:, h * DN_HEAD:(h + 1) * DN_HEAD]
        k = y[:, (DN_HEADS + h) * DN_HEAD:(DN_HEADS + h + 1) * DN_HEAD]
        v = y[:, (2 * DN_HEADS + h) * DN_HEAD:(2 * DN_HEADS + h + 1) * DN_HEAD]
        q = q * lax.rsqrt(jnp.sum(q * q, axis=-1, keepdims=True) + EPS) * (DN_HEAD ** -0.5)
        k = k * lax.rsqrt(jnp.sum(k * k, axis=-1, keepdims=True) + EPS)
        hd = dir_off + h
        beta = jnp.broadcast_to(beta_all[:, hd:hd + 1], (n, LANES))
        g = jnp.broadcast_to(g_all[:, 2 * DN_HEADS + hd:2 * DN_HEADS + hd + 1], (n, LANES))

        gc = None
        for part in _split3(g):
            t = _dot(incl_bf, part)
            gc = t if gc is None else gc + t
        g_row = jnp.transpose(gc)[0:1, :]
        diff = jnp.concatenate([gc, gc], axis=1) - g_row
        decay = jnp.where(incl, jnp.exp(jnp.where(incl, diff, 0.0)), 0.0)

        kb = k * beta
        k_bf = k.astype(BF16)
        a_mat = jnp.where(strict, _dot_nt(kb.astype(BF16), k_bf) * decay, 0.0)
        x_inv = eye - jnp.where(level_mask(0), a_mat, 0.0)
        for lb in range(1, 6):
            e_b = jnp.where(level_mask(lb), a_mat, 0.0).astype(BF16)
            x_bf = x_inv.astype(BF16)
            x_inv = x_inv - _dot(_dot(x_bf, e_b).astype(BF16), x_bf)

        e_gc = jnp.exp(gc)
        rhs = jnp.concatenate([v * beta, kb * e_gc], axis=1).astype(BF16)
        sol = _dot(x_inv.astype(BF16), rhs)
        u = sol[:, :DN_HEAD]
        w = sol[:, DN_HEAD:].astype(BF16)
        qk = jnp.where(incl, _dot_nt(q.astype(BF16), k_bf) * decay, 0.0).astype(BF16)
        qd = (q * e_gc).astype(BF16)

        gl_rows = []
        for c in range(n // DN_CHUNK):
            r = c * DN_CHUNK + (0 if rev else DN_CHUNK - 1)
            gl_rows.append(gc[r:r + 1, :])
        gl = jnp.concatenate(
            [jnp.broadcast_to(t, (DN_CHUNK, LANES)) for t in gl_rows], axis=0)
        kd_t = jnp.transpose(k * jnp.exp(gl - gc)).astype(BF16)

        state = s_ref[h]
        o_parts = [None] * (n // DN_CHUNK)
        for c in order:
            lo, hi = c * DN_CHUNK, (c + 1) * DN_CHUNK
            s_bf = state.astype(BF16)
            vn = u[lo:hi] - _dot(w[lo:hi], s_bf)
            pieces = []
            if lo:
                pieces.append(jnp.zeros((lo, LANES), F32))
            pieces.append(vn)
            if n - hi:
                pieces.append(jnp.zeros((n - hi, LANES), F32))
            vn_pad = jnp.concatenate(pieces, axis=0).astype(BF16)
            o_parts[c] = _dot(qd[lo:hi], s_bf) + _dot(qk[lo:hi], vn_pad)
            state = state * jnp.exp(gl_rows[c]) + _dot(kd_t, vn_pad)
        s_ref[h] = state
        o_ref[0, :, h * DN_HEAD:(h + 1) * DN_HEAD] = jnp.concatenate(o_parts, axis=0)


def _deltanet(dqkv, ba, conv_w, alog_pad, dtb_pad, rev):
    b, s, _ = dqkv.shape
    nblk = s // DN_BLOCK
    per = DN_BLOCK // CONV_HALO
    nhalo = s // CONV_HALO

    def pos(c):
        return (nblk - 1 - c) if rev else c

    return pl.pallas_call(
        functools.partial(_dn_kernel, rev),
        grid=(b, nblk),
        in_specs=[
            pl.BlockSpec((1, DN_BLOCK, DN_QKV_W), lambda bi, c: (bi, pos(c), 0)),
            pl.BlockSpec((1, CONV_HALO, DN_QKV_W),
                         lambda bi, c: (bi, jnp.maximum(pos(c) * per - 1, 0), 0)),
            pl.BlockSpec((1, CONV_HALO, DN_QKV_W),
                         lambda bi, c: (bi, jnp.minimum((pos(c) + 1) * per, nhalo - 1), 0)),
            pl.BlockSpec((1, DN_BLOCK, LANES), lambda bi, c: (bi, pos(c), 0)),
            pl.BlockSpec((CONV_W, DN_QKV_W), lambda bi, c: (0, 0)),
            pl.BlockSpec((1, LANES), lambda bi, c: (0, 0)),
            pl.BlockSpec((1, LANES), lambda bi, c: (0, 0)),
        ],
        out_specs=pl.BlockSpec((1, DN_BLOCK, DN_V_W), lambda bi, c: (bi, pos(c), 0)),
        out_shape=jax.ShapeDtypeStruct((b, s, DN_V_W), F32),
        scratch_shapes=[pltpu.VMEM((DN_HEADS, DN_HEAD, DN_HEAD), F32)],
        compiler_params=pltpu.CompilerParams(
            dimension_semantics=("parallel", "arbitrary"), vmem_limit_bytes=VMEM_LIMIT),
        name="deltanet_bwd" if rev else "deltanet_fwd",
    )(dqkv, dqkv, dqkv, ba, conv_w, alog_pad, dtb_pad)


def _post_kernel(ff_chunk, final_norm, x_ref, attn_ref, of_ref, ob_ref, dz_ref, p_ref, dng_ref, wout_ref,
                 nffn_ref, wg_ref, wu_ref, wd_ref, nple_ref, wpg_ref, wple_ref, nfin_ref, out_ref):
    o = of_ref[...] + ob_ref[...]
    z = dz_ref[...].astype(F32)
    parts = [attn_ref[...]]
    for h in range(DN_HEADS):
        sl = slice(h * DN_HEAD, (h + 1) * DN_HEAD)
        zh = z[:, sl]
        parts.append((_rms(o[:, sl], dng_ref[...]) * (zh * _sigmoid(zh))).astype(BF16))
    mixed = jnp.concatenate(parts, axis=1)
    h1 = x_ref[...] + _dot(mixed, wout_ref[...])

    hn = _rms(h1, nffn_ref[...]).astype(BF16)
    h2 = h1
    ff = wg_ref.shape[1]
    for c in range(ff // ff_chunk):
        sl = slice(c * ff_chunk, (c + 1) * ff_chunk)
        gate = _dot(hn, wg_ref[:, sl])
        up = _dot(hn, wu_ref[:, sl])
        act = (gate * _sigmoid(gate) * up).astype(BF16)
        h2 = h2 + _dot(act, wd_ref[sl, :])

    hp = _rms(h2, nple_ref[...]).astype(BF16)
    gate = _sigmoid(_dot(hp, wpg_ref[...]))
    h3 = h2 + gate * _dot(p_ref[...].astype(BF16), wple_ref[...])
    out_ref[...] = _rms(h3, nfin_ref[...]) if final_norm else h3


def _post(xf, attn, o_f, o_b, dz, pf, dng, w_out, nffn, w_gate, w_up, w_down, nple, w_pg, w_ple,
          nfin, tm, ff_chunk, final_norm):
    t, d = xf.shape
    row = lambda i: (i, 0)
    const = lambda i: (0, 0)

    def resident(shape):
        return pl.BlockSpec(shape, const, pipeline_mode=pl.Buffered(1))

    ff = w_gate.shape[1]
    return pl.pallas_call(
        functools.partial(_post_kernel, ff_chunk, final_norm),
        grid=(t // tm,),
        in_specs=[
            pl.BlockSpec((tm, d), row),
            pl.BlockSpec((tm, ATTN_Q_W), row),
            pl.BlockSpec((tm, DN_V_W), row),
            pl.BlockSpec((tm, DN_V_W), row),
            pl.BlockSpec((tm, DN_V_W), row),
            pl.BlockSpec((tm, pf.shape[1]), row),
            resident((1, DN_HEAD)),
            resident((ATTN_Q_W + DN_V_W, d)),
            resident((1, d)),
            resident((d, ff)),
            resident((d, ff)),
            resident((ff, d)),
            resident((1, d)),
            resident((d, d)),
            resident((pf.shape[1], d)),
            resident((1, d)),
        ],
        out_specs=pl.BlockSpec((tm, d), row),
        out_shape=jax.ShapeDtypeStruct((t, d), F32),
        compiler_params=pltpu.CompilerParams(
            dimension_semantics=("parallel",), vmem_limit_bytes=VMEM_LIMIT),
        name="post",
    )(xf, attn, o_f, o_b, dz, pf, dng, w_out, nffn, w_gate, w_up, w_down, nple, w_pg, w_ple, nfin)


def _rope_tables(seq):
    pos = jnp.arange(seq, dtype=jnp.int32)
    rowcol = jnp.stack([(pos // GRID_W).astype(F32), (pos % GRID_W).astype(F32)], axis=1)
    sec = ATTN_HEAD_DIM // 2
    inv_freq = ROPE_THETA ** (-jnp.arange(0, sec, 2, dtype=F32) / sec)
    d = np.arange(LANES) % ATTN_HEAD_DIM
    axis = d // sec
    idx = d % (sec // 2)
    sign = np.where((d % sec) < sec // 2, -1.0, 1.0).astype(np.float32)
    ang = rowcol[:, axis] * inv_freq[idx][None, :]
    return jnp.cos(ang), jnp.sin(ang) * sign[None, :]


def kernel(x, p, norm_mix, w_in, conv_w, q_norm, k_norm, a_log, dt_bias, dn_norm, w_out,
           norm_ffn, w_gate, w_up, w_down, norm_ple, w_ple_gate, w_ple, norm_final):
    b, s, d = x.shape
    depth = w_in.shape[0]
    t = b * s
    tm, tq, tk = min(ROW_TILE, s), min(Q_TILE, s), min(KV_TILE, s)
    cos_t, sin_t = _rope_tables(s)

    h = x.reshape(t, d)
    for i in range(depth):
        wi = w_in[i].astype(BF16)
        c = 0
        wq = wi[:, c:c + ATTN_Q_W]; c += ATTN_Q_W
        wk = wi[:, c:c + KV_W]; c += KV_W
        wv = wi[:, c:c + KV_W]; c += KV_W
        wdn = wi[:, c:c + DN_QKV_W]; c += DN_QKV_W
        wz = wi[:, c:c + DN_V_W]; c += DN_V_W
        wba = wi[:, c:]
        wq = wq.reshape(d, ATTN_HEADS, ATTN_HEAD_DIM)
        zeros = jnp.zeros_like(wq)
        lower = (np.arange(ATTN_HEADS) // ATTN_GROUP == 0)[None, :, None]
        wq_exp = jnp.concatenate(
            [jnp.where(lower, wq, zeros), jnp.where(lower, zeros, wq)], axis=-1).reshape(d, Q_EXP_W)
        wba = jnp.pad(wba, ((0, 0), (0, LANES - wba.shape[1])))
        w_cat = jnp.concatenate([wq_exp, wk, wv, wdn, wz, wba], axis=1)

        qg = jnp.tile(q_norm[i], LANES // ATTN_HEAD_DIM)[None, :]
        kg = jnp.tile(k_norm[i], LANES // ATTN_HEAD_DIM)[None, :]
        q, k, va, vb, dqkv, dz, ba = _inproj(h, norm_mix[i][None, :], w_cat, cos_t, sin_t, qg, kg, s, tm)

        score_bound = (ATTN_HEAD_DIM ** 0.5) * jnp.max(jnp.abs(q_norm[i])) * jnp.max(jnp.abs(k_norm[i]))
        bounded = (score_bound <= MAX_UNSHIFTED_SCORE).astype(jnp.int32).reshape(1)
        attn = _attention(bounded, q.reshape(b, s, Q_EXP_W), k.reshape(b, s, KV_W),
                          va.reshape(b, s, KV_W), vb.reshape(b, s, KV_W), tq, tk)

        pad = (2 * DN_HEADS, LANES - 4 * DN_HEADS)
        alog_pad = jnp.pad(a_log[i], pad)[None, :]
        dtb_pad = jnp.pad(dt_bias[i], pad)[None, :]
        dqkv3 = dqkv.reshape(b, s, DN_QKV_W)
        ba3 = ba.reshape(b, s, LANES)
        o_f = _deltanet(dqkv3, ba3, conv_w[i], alog_pad, dtb_pad, False)
        o_b = _deltanet(dqkv3, ba3, conv_w[i], alog_pad, dtb_pad, True)

        h = _post(h, attn.reshape(t, ATTN_Q_W), o_f.reshape(t, DN_V_W), o_b.reshape(t, DN_V_W), dz,
                  p[i].reshape(t, -1), dn_norm[i][None, :], w_out[i].astype(BF16),
                  norm_ffn[i][None, :], w_gate[i].astype(BF16), w_up[i].astype(BF16),
                  w_down[i].astype(BF16), norm_ple[i][None, :], w_ple_gate[i].astype(BF16),
                  w_ple[i].astype(BF16),
                  norm_final[None, :], tm, FF_CHUNK, i == depth - 1)
    return h.reshape(b, s, d)
```

```python
import functools

import numpy as np
import jax
import jax.numpy as jnp
from jax import lax
from jax.experimental import pallas as pl
from jax.experimental.pallas import tpu as pltpu

F32 = jnp.float32
BF16 = jnp.bfloat16

LANES = 128
SUBLANES = 8
VMEM_LIMIT = 60000 * 1024

GRID_W = 64
ATTN_HEADS = 8
ATTN_KV_HEADS = 2
ATTN_HEAD_DIM = 64
ATTN_GROUP = ATTN_HEADS // ATTN_KV_HEADS
ROPE_THETA = 10000.0
DN_HEADS = 4
DN_HEAD = 128
DN_CHUNK = 128
DN_DIRS = 2
DN_LEVELS = 7
N_MASKS = 2 + DN_LEVELS
CONV_W = 5
EPS = 1e-6
LOG2E = float(np.log2(np.e))

Q_EXP_W = ATTN_HEADS * LANES
KV_W = ATTN_KV_HEADS * ATTN_HEAD_DIM
DN_QKV_W = 3 * DN_HEADS * DN_HEAD
DN_V_W = DN_HEADS * DN_HEAD
ATTN_Q_W = ATTN_HEADS * ATTN_HEAD_DIM

ROW_TILE = 512
Q_TILE = 256
KV_TILE = 1024
Q_SCALE = ATTN_HEAD_DIM ** -0.5 * LOG2E
MAX_UNSHIFTED_SCORE = 40.0
FF_CHUNK = 256
DN_BLOCK = 256
DN_SUB = LANES
CONV_HALO = SUBLANES


def _dot(a, b):
    return jnp.dot(a, b, preferred_element_type=F32)


def _dot_nt(a, b):
    return lax.dot_general(a, b, (((1,), (1,)), ((), ())), preferred_element_type=F32)


def _rms(x, gain):
    return x * lax.rsqrt(jnp.mean(x * x, axis=-1, keepdims=True) + EPS) * gain


def _sigmoid(x):
    return 1.0 / (1.0 + jnp.exp(-x))


def _inproj_kernel(x_ref, gain_ref, w_ref, cos_ref, sin_ref, qg_ref, kg_ref,
                   q_ref, k_ref, va_ref, vb_ref, dqkv_ref, dz_ref, ba_ref):
    tm = x_ref.shape[0]
    hb = _rms(x_ref[...], gain_ref[...]).astype(BF16)
    cos = cos_ref[...]
    sin = sin_ref[...]
    lane = lax.broadcasted_iota(jnp.int32, (tm, LANES), 1)
    low16 = (lane % 32) < 16

    def rope(t):
        partner = jnp.where(low16, pltpu.roll(t, LANES - 16, 1), pltpu.roll(t, 16, 1))
        return t * cos + partner * sin

    c0 = 0
    q_all = _dot(hb, w_ref[:, c0:c0 + Q_EXP_W])
    for h in range(ATTN_HEADS):
        qh = q_all[:, h * LANES:(h + 1) * LANES]
        ss = jnp.sum(qh * qh, axis=-1, keepdims=True) * (1.0 / ATTN_HEAD_DIM)
        qn = qh * lax.rsqrt(ss + EPS) * qg_ref[...]
        q_ref[:, h * LANES:(h + 1) * LANES] = (rope(qn) * Q_SCALE).astype(BF16)
    c0 += Q_EXP_W

    kv = _dot(hb, w_ref[:, c0:c0 + 2 * KV_W])
    kk = kv[:, :KV_W]
    head0 = lane < ATTN_HEAD_DIM
    k2 = kk * kk
    ss0 = jnp.sum(jnp.where(head0, k2, 0.0), axis=-1, keepdims=True) * (1.0 / ATTN_HEAD_DIM)
    ss1 = jnp.sum(jnp.where(head0, 0.0, k2), axis=-1, keepdims=True) * (1.0 / ATTN_HEAD_DIM)
    rs = jnp.where(head0, lax.rsqrt(ss0 + EPS), lax.rsqrt(ss1 + EPS))
    k_ref[...] = rope(kk * rs * kg_ref[...]).astype(BF16)
    vv = kv[:, KV_W:]
    va_ref[...] = jnp.where(head0, vv, 1.0).astype(BF16)
    vb_ref[...] = jnp.where(head0, 1.0, vv).astype(BF16)
    c0 += 2 * KV_W

    dqkv_ref[...] = _dot(hb, w_ref[:, c0:c0 + DN_QKV_W]).astype(BF16)
    c0 += DN_QKV_W
    zb = _dot(hb, w_ref[:, c0:c0 + DN_V_W + LANES])
    dz_ref[...] = zb[:, :DN_V_W].astype(BF16)
    ba_ref[...] = zb[:, DN_V_W:]


def _inproj(xf, gain, w_cat, cos_t, sin_t, qg, kg, seq, tm):
    t, d = xf.shape
    nseq = seq // tm
    wcols = w_cat.shape[1]
    row = lambda i: (i, 0)
    const = lambda i: (0, 0)
    tab = lambda i: (i % nseq, 0)
    return pl.pallas_call(
        _inproj_kernel,
        grid=(t // tm,),
        in_specs=[
            pl.BlockSpec((tm, d), row),
            pl.BlockSpec((1, d), const),
            pl.BlockSpec((d, wcols), const),
            pl.BlockSpec((tm, LANES), tab),
            pl.BlockSpec((tm, LANES), tab),
            pl.BlockSpec((1, LANES), const),
            pl.BlockSpec((1, LANES), const),
        ],
        out_specs=[
            pl.BlockSpec((tm, Q_EXP_W), row),
            pl.BlockSpec((tm, KV_W), row),
            pl.BlockSpec((tm, KV_W), row),
            pl.BlockSpec((tm, KV_W), row),
            pl.BlockSpec((tm, DN_QKV_W), row),
            pl.BlockSpec((tm, DN_V_W), row),
            pl.BlockSpec((tm, LANES), row),
        ],
        out_shape=[
            jax.ShapeDtypeStruct((t, Q_EXP_W), BF16),
            jax.ShapeDtypeStruct((t, KV_W), BF16),
            jax.ShapeDtypeStruct((t, KV_W), BF16),
            jax.ShapeDtypeStruct((t, KV_W), BF16),
            jax.ShapeDtypeStruct((t, DN_QKV_W), BF16),
            jax.ShapeDtypeStruct((t, DN_V_W), BF16),
            jax.ShapeDtypeStruct((t, LANES), F32),
        ],
        compiler_params=pltpu.CompilerParams(
            dimension_semantics=("parallel",), vmem_limit_bytes=VMEM_LIMIT),
        name="inproj",
    )(xf, gain, w_cat, cos_t, sin_t, qg, kg)


def _attn_kernel(tk, bounded_ref, q_ref, k_ref, va_ref, vb_ref, o_ref, qs_ref, acc_ref, m_ref):
    tq = q_ref.shape[1]
    nkv = k_ref.shape[1] // tk
    for g in range(ATTN_KV_HEADS):
        for j in range(ATTN_GROUP):
            h = g * ATTN_GROUP + j
            qs_ref[g, j * tq:(j + 1) * tq, :] = q_ref[0, :, h * LANES:(h + 1) * LANES]
    acc_ref[...] = jnp.zeros(acc_ref.shape, F32)

    def tiles(i):
        off = pl.multiple_of(i * tk, tk)
        kt = k_ref[0, pl.ds(off, tk), :]
        return kt, (va_ref[0, pl.ds(off, tk), :], vb_ref[0, pl.ds(off, tk), :])

    @pl.when(bounded_ref[0] == 1)
    def _():
        def body(i, carry):
            kt, vt = tiles(i)
            for g in range(ATTN_KV_HEADS):
                p = jnp.exp2(_dot_nt(qs_ref[g], kt)).astype(BF16)
                acc_ref[g] += _dot(p, vt[g])
            return carry

        lax.fori_loop(0, nkv, body, 0)

    @pl.when(bounded_ref[0] == 0)
    def _():
        m_ref[...] = jnp.full(m_ref.shape, -jnp.inf, F32)

        def body(i, carry):
            kt, vt = tiles(i)
            for g in range(ATTN_KV_HEADS):
                s = _dot_nt(qs_ref[g], kt)
                m_prev = m_ref[g]
                m_new = jnp.maximum(m_prev, jnp.max(s, axis=-1, keepdims=True))
                p = jnp.exp2(s - m_new[:, :1]).astype(BF16)
                acc_ref[g] = jnp.exp2(m_prev - m_new) * acc_ref[g] + _dot(p, vt[g])
                m_ref[g] = m_new
            return carry

        lax.fori_loop(0, nkv, body, 0)

    lane = lax.broadcasted_iota(jnp.int32, (tq, LANES), 1)
    lower = lane < ATTN_HEAD_DIM
    for g in range(ATTN_KV_HEADS):
        for jp in range(ATTN_GROUP // 2):
            halves = []
            for j in (2 * jp, 2 * jp + 1):
                a = acc_ref[g, j * tq:(j + 1) * tq, :]
                halves.append(a / pltpu.roll(a, ATTN_HEAD_DIM, 1))
            if g == 0:
                out = jnp.where(lower, halves[0], pltpu.roll(halves[1], ATTN_HEAD_DIM, 1))
            else:
                out = jnp.where(lower, pltpu.roll(halves[0], ATTN_HEAD_DIM, 1), halves[1])
            c = g * (ATTN_GROUP // 2) + jp
            o_ref[0, :, c * LANES:(c + 1) * LANES] = out.astype(BF16)


def _attention(bounded, q, k, va, vb, tq, tk):
    b, s, _ = q.shape
    kv_spec = pl.BlockSpec((1, s, KV_W), lambda bi, qi, flag: (bi, 0, 0))
    return pl.pallas_call(
        functools.partial(_attn_kernel, tk),
        grid_spec=pltpu.PrefetchScalarGridSpec(
            num_scalar_prefetch=1,
            grid=(b, s // tq),
            in_specs=[
                pl.BlockSpec((1, tq, Q_EXP_W), lambda bi, qi, flag: (bi, qi, 0)),
                kv_spec, kv_spec, kv_spec,
            ],
            out_specs=pl.BlockSpec((1, tq, ATTN_Q_W), lambda bi, qi, flag: (bi, qi, 0)),
            scratch_shapes=[
                pltpu.VMEM((ATTN_KV_HEADS, ATTN_GROUP * tq, LANES), BF16),
                pltpu.VMEM((ATTN_KV_HEADS, ATTN_GROUP * tq, LANES), F32),
                pltpu.VMEM((ATTN_KV_HEADS, ATTN_GROUP * tq, LANES), F32),
            ],
        ),
        out_shape=jax.ShapeDtypeStruct((b, s, ATTN_Q_W), BF16),
        compiler_params=pltpu.CompilerParams(
            dimension_semantics=("parallel", "arbitrary"), vmem_limit_bytes=VMEM_LIMIT),
        name="attention",
    )(bounded, q, k, va, vb)


def _dn_masks():
    n = DN_SUB
    row = np.arange(n)[:, None]
    col = np.arange(n)[None, :]
    same = (row // DN_CHUNK) == (col // DN_CHUNK)
    out = []
    for rev in (False, True):
        a, b = (col, row) if rev else (row, col)
        out.append(same & (a >= b))
        out.append(same & (a > b))
        for lb in range(DN_LEVELS):
            out.append(((a >> lb) == (b >> lb) + 1) & (((a >> lb) & 1) == 1))
    return np.stack(out).astype(np.float32)


def _dn_local_kernel(qkv_ref, prev_ref, next_ref, ba_ref, convw_ref, alog_ref, dtb_ref, mask_ref,
                     wq_ref, u_ref, ot_ref, kdt_ref, gt_ref):
    n = DN_BLOCK
    nch = n // DN_CHUNK
    blk = pl.program_id(1)
    nblk = pl.num_programs(1)

    top = jnp.where(blk == 0, 0.0, prev_ref[0].astype(F32))
    bot = jnp.where(blk == nblk - 1, 0.0, next_ref[0].astype(F32))
    ext = jnp.concatenate([top, qkv_ref[0].astype(F32), bot], axis=0)
    rows = n + 2 * CONV_HALO
    acc = None
    for j in range(CONV_W):
        shift = (CONV_W // 2 - j) % rows
        xs = ext if shift == 0 else pltpu.roll(ext, shift, 0)
        term = xs[CONV_HALO:CONV_HALO + n] * convw_ref[j:j + 1, :]
        acc = term if acc is None else acc + term
    y = acc * _sigmoid(acc)

    ba = ba_ref[0]
    beta_all = _sigmoid(ba)
    sp_arg = ba + dtb_ref[...]
    softplus = jnp.maximum(sp_arg, 0.0) + jnp.log(1.0 + jnp.exp(-jnp.abs(sp_arg)))
    g2 = (-LOG2E) * jnp.exp(alog_ref[...]) * softplus

    in_chunk = lax.broadcasted_iota(jnp.int32, (n, LANES), 0) % DN_CHUNK
    g_fwd = g2
    shift = 1
    while shift < DN_CHUNK:
        g_fwd = g_fwd + jnp.where(in_chunk >= shift, pltpu.roll(g_fwd, shift, 0), 0.0)
        shift *= 2
    total = jnp.concatenate(
        [jnp.broadcast_to(g_fwd[(c + 1) * DN_CHUNK - 1:(c + 1) * DN_CHUNK, :], (DN_CHUNK, LANES))
         for c in range(nch)], axis=0)
    g_bwd = total - g_fwd + g2
    g_cols = (g_fwd, g_bwd)
    g_rows = (jnp.transpose(g_fwd), jnp.transpose(g_bwd))

    m = DN_SUB
    row = lax.broadcasted_iota(jnp.int32, (m, m), 0)
    col = lax.broadcasted_iota(jnp.int32, (m, m), 1)
    eye = jnp.where(row == col, 1.0, 0.0)
    subs = [slice(s * m, (s + 1) * m) for s in range(n // m)]

    qs, ks, vs, kk, qk_raw = [], [], [], [], []
    for h in range(DN_HEADS):
        q = y[:, h * DN_HEAD:(h + 1) * DN_HEAD]
        k = y[:, (DN_HEADS + h) * DN_HEAD:(DN_HEADS + h + 1) * DN_HEAD]
        q = q * lax.rsqrt(jnp.sum(q * q, axis=-1, keepdims=True) + EPS) * (DN_HEAD ** -0.5)
        k = k * lax.rsqrt(jnp.sum(k * k, axis=-1, keepdims=True) + EPS)
        k_bf = k.astype(BF16)
        q_bf = q.astype(BF16)
        qs.append(q)
        ks.append(k)
        vs.append(y[:, (2 * DN_HEADS + h) * DN_HEAD:(2 * DN_HEADS + h + 1) * DN_HEAD])
        kk.append([_dot_nt(k_bf[r], k_bf[r]) for r in subs])
        qk_raw.append([_dot_nt(q_bf[r], k_bf[r]) for r in subs])

    hds = [(d, h) for d in range(DN_DIRS) for h in range(DN_HEADS)]
    prob = [(i, s) for i in range(len(hds)) for s in range(len(subs))]
    a_bf, x_inv, qk_bf, rhs_bf, qd, g_tot = {}, {}, {}, [], [], []
    for i, (d, h) in enumerate(hds):
        lane = 2 * DN_HEADS + i
        gc = jnp.broadcast_to(g_cols[d][:, lane:lane + 1], (n, LANES))
        beta = jnp.broadcast_to(beta_all[:, i:i + 1], (n, LANES))
        for s, r in enumerate(subs):
            g_row = g_rows[d][lane:lane + 1, r]
            decay = jnp.exp2(jnp.minimum(gc[r] - g_row, 0.0))
            a_mat = kk[h][s] * beta[r] * (decay * mask_ref[d * N_MASKS + 1])
            a_bf[i, s] = a_mat.astype(BF16)
            x_inv[i, s] = eye - a_mat * mask_ref[d * N_MASKS + 2]
            qk_bf[i, s] = (qk_raw[h][s] * (decay * mask_ref[d * N_MASKS])).astype(BF16)
        e_gc = jnp.exp2(gc)
        rhs_bf.append(jnp.concatenate([vs[h] * beta, ks[h] * beta * e_gc], axis=1).astype(BF16))
        qd.append(qs[h] * e_gc)

        edge = 0 if d else DN_CHUNK - 1
        gl_rows = [gc[c * DN_CHUNK + edge:c * DN_CHUNK + edge + 1, :] for c in range(nch)]
        gl = jnp.concatenate([jnp.broadcast_to(t, (DN_CHUNK, LANES)) for t in gl_rows], axis=0)
        g_tot.append([jnp.exp2(t) for t in gl_rows])
        kdt_ref[0, i] = jnp.transpose(ks[h] * jnp.exp2(gl - gc)).astype(BF16)

    for lb in range(1, DN_LEVELS):
        for i, s in prob:
            x_bf = x_inv[i, s].astype(BF16)
            upd = _dot(_dot(x_bf, a_bf[i, s]).astype(BF16), x_bf)
            x_inv[i, s] = x_inv[i, s] - upd * mask_ref[hds[i][0] * N_MASKS + 2 + lb]

    sols = {(i, s): _dot(x_inv[i, s].astype(BF16), rhs_bf[i][subs[s]]) for i, s in prob}
    corrs = {(i, s): _dot(qk_bf[i, s], sols[i, s].astype(BF16)) for i, s in prob}
    for i in range(len(hds)):
        sol = jnp.concatenate([sols[i, s] for s in range(len(subs))], axis=0)
        corr = jnp.concatenate([corrs[i, s] for s in range(len(subs))], axis=0)
        w_bf = sol[:, DN_HEAD:].astype(BF16)
        qe_bf = (qd[i] - corr[:, DN_HEAD:]).astype(BF16)
        pieces = []
        for c in range(nch):
            sl = slice(c * DN_CHUNK, (c + 1) * DN_CHUNK)
            pieces += [w_bf[sl], qe_bf[sl]]
        wq_ref[0, i] = jnp.concatenate(pieces, axis=0)
        u_ref[0, i] = sol[:, :DN_HEAD].astype(BF16)
        ot_ref[0, i] = corr[:, :DN_HEAD].astype(BF16)

    for c in range(nch):
        gt_ref[0, c] = jnp.concatenate([t[c] for t in g_tot], axis=0)


def _dn_local(dqkv, ba, conv_w, alog_pad, dtb_pad, masks):
    b, s, _ = dqkv.shape
    n = DN_BLOCK
    nblk = s // n
    per = n // CONV_HALO
    nhalo = s // CONV_HALO
    hds = DN_DIRS * DN_HEADS
    return pl.pallas_call(
        _dn_local_kernel,
        grid=(b, nblk),
        in_specs=[
            pl.BlockSpec((1, n, DN_QKV_W), lambda bi, c: (bi, c, 0)),
            pl.BlockSpec((1, CONV_HALO, DN_QKV_W), lambda bi, c: (bi, jnp.maximum(c * per - 1, 0), 0)),
            pl.BlockSpec((1, CONV_HALO, DN_QKV_W),
                         lambda bi, c: (bi, jnp.minimum((c + 1) * per, nhalo - 1), 0)),
            pl.BlockSpec((1, n, LANES), lambda bi, c: (bi, c, 0)),
            pl.BlockSpec((CONV_W, DN_QKV_W), lambda bi, c: (0, 0)),
            pl.BlockSpec((1, LANES), lambda bi, c: (0, 0)),
            pl.BlockSpec((1, LANES), lambda bi, c: (0, 0)),
            pl.BlockSpec(masks.shape, lambda bi, c: (0, 0, 0)),
        ],
        out_specs=[
            pl.BlockSpec((1, hds, 2 * n, DN_HEAD), lambda bi, c: (bi, 0, c, 0)),
            pl.BlockSpec((1, hds, n, DN_HEAD), lambda bi, c: (bi, 0, c, 0)),
            pl.BlockSpec((1, hds, n, DN_HEAD), lambda bi, c: (bi, 0, c, 0)),
            pl.BlockSpec((1, hds, DN_HEAD, n), lambda bi, c: (bi, 0, 0, c)),
            pl.BlockSpec((1, n // DN_CHUNK, hds, LANES), lambda bi, c: (bi, c, 0, 0)),
        ],
        out_shape=[
            jax.ShapeDtypeStruct((b, hds, 2 * s, DN_HEAD), BF16),
            jax.ShapeDtypeStruct((b, hds, s, DN_HEAD), BF16),
            jax.ShapeDtypeStruct((b, hds, s, DN_HEAD), BF16),
            jax.ShapeDtypeStruct((b, hds, DN_HEAD, s), BF16),
            jax.ShapeDtypeStruct((b, s // DN_CHUNK, hds, LANES), F32),
        ],
        compiler_params=pltpu.CompilerParams(
            dimension_semantics=("parallel", "parallel"), vmem_limit_bytes=VMEM_LIMIT),
        name="deltanet_local",
    )(dqkv, dqkv, dqkv, ba, conv_w, alog_pad, dtb_pad, masks)


def _dn_scan_kernel(wqf_ref, wqb_ref, uf_ref, ub_ref, otf_ref, otb_ref, kdf_ref, kdb_ref,
                    gtf_ref, gtb_ref, of_ref, ob_ref, s_ref):
    n = DN_BLOCK
    nch = n // DN_CHUNK

    @pl.when(pl.program_id(1) == 0)
    def _():
        s_ref[...] = jnp.zeros(s_ref.shape, F32)

    refs = ((wqf_ref, uf_ref, otf_ref, kdf_ref, gtf_ref), (wqb_ref, ub_ref, otb_ref, kdb_ref, gtb_ref))
    hds = DN_DIRS * DN_HEADS
    states = [s_ref[hd] for hd in range(hds)]
    outs = [[None] * nch for _ in range(hds)]
    for r in range(nch):
        for d in range(DN_DIRS):
            wq_ref, u_ref, ot_ref, kd_ref, gt_ref = refs[d]
            c = nch - 1 - r if d else r
            rows = slice(c * DN_CHUNK, (c + 1) * DN_CHUNK)
            for h in range(DN_HEADS):
                hd = d * DN_HEADS + h
                s_bf = states[hd].astype(BF16)
                res = _dot(wq_ref[0, h, 2 * c * DN_CHUNK:2 * (c + 1) * DN_CHUNK, :], s_bf)
                vn = u_ref[0, h, rows, :].astype(F32) - res[:DN_CHUNK]
                outs[hd][c] = res[DN_CHUNK:] + ot_ref[0, h, rows, :].astype(F32)
                pieces = []
                if c:
                    pieces.append(jnp.zeros((c * DN_CHUNK, LANES), BF16))
                pieces.append(vn.astype(BF16))
                if nch - 1 - c:
                    pieces.append(jnp.zeros(((nch - 1 - c) * DN_CHUNK, LANES), BF16))
                vn_pad = jnp.concatenate(pieces, axis=0)
                states[hd] = states[hd] * gt_ref[0, c, hd:hd + 1, :] + _dot(kd_ref[0, h], vn_pad)
    for hd in range(hds):
        s_ref[hd] = states[hd]
    for d, o_ref in enumerate((of_ref, ob_ref)):
        for h in range(DN_HEADS):
            o_ref[0, :, h * DN_HEAD:(h + 1) * DN_HEAD] = jnp.concatenate(
                outs[d * DN_HEADS + h], axis=0).astype(BF16)


def _dn_scan(wq, u, ot, kdt, gt):
    b, hds, s, _ = u.shape
    n = DN_BLOCK
    nblk = s // n
    nch = n // DN_CHUNK
    fwd = lambda bi, c: c
    bwd = lambda bi, c: nblk - 1 - c

    def pair(shape, index):
        return [pl.BlockSpec(shape, functools.partial(index, 0, fwd)),
                pl.BlockSpec(shape, functools.partial(index, 1, bwd))]

    rows = lambda d, pos, bi, c: (bi, d, pos(bi, c), 0)
    lanes = lambda d, pos, bi, c: (bi, d, 0, pos(bi, c))
    gts = lambda d, pos, bi, c: (bi, pos(bi, c), 0, 0)
    return pl.pallas_call(
        _dn_scan_kernel,
        grid=(b, nblk),
        in_specs=(pair((1, DN_HEADS, 2 * n, DN_HEAD), rows) + pair((1, DN_HEADS, n, DN_HEAD), rows)
                  + pair((1, DN_HEADS, n, DN_HEAD), rows) + pair((1, DN_HEADS, DN_HEAD, n), lanes)
                  + pair((1, nch, hds, LANES), gts)),
        out_specs=[
            pl.BlockSpec((1, n, DN_V_W), lambda bi, c: (bi, c, 0)),
            pl.BlockSpec((1, n, DN_V_W), lambda bi, c: (bi, nblk - 1 - c, 0)),
        ],
        out_shape=[jax.ShapeDtypeStruct((b, s, DN_V_W), BF16)] * 2,
        scratch_shapes=[pltpu.VMEM((hds, DN_HEAD, DN_HEAD), F32)],
        compiler_params=pltpu.CompilerParams(
            dimension_semantics=("parallel", "arbitrary"), vmem_limit_bytes=VMEM_LIMIT),
        name="deltanet_scan",
    )(wq, wq, u, u, ot, ot, kdt, kdt, gt, gt)


def _post_kernel(ff_chunk, final_norm, x_ref, attn_ref, of_ref, ob_ref, dz_ref, p_ref, dng_ref, wout_ref,
                 nffn_ref, wg_ref, wu_ref, wd_ref, nple_ref, wpg_ref, wple_ref, nfin_ref, out_ref):
    o = of_ref[...].astype(F32) + ob_ref[...].astype(F32)
    z = dz_ref[...].astype(F32)
    parts = [attn_ref[...]]
    for h in range(DN_HEADS):
        sl = slice(h * DN_HEAD, (h + 1) * DN_HEAD)
        zh = z[:, sl]
        parts.append((_rms(o[:, sl], dng_ref[...]) * (zh * _sigmoid(zh))).astype(BF16))
    mixed = jnp.concatenate(parts, axis=1)
    h1 = x_ref[...] + _dot(mixed, wout_ref[...])

    hn = _rms(h1, nffn_ref[...]).astype(BF16)
    h2 = h1
    ff = wg_ref.shape[1]
    for c in range(ff // ff_chunk):
        sl = slice(c * ff_chunk, (c + 1) * ff_chunk)
        gate = _dot(hn, wg_ref[:, sl])
        up = _dot(hn, wu_ref[:, sl])
        act = (gate * _sigmoid(gate) * up).astype(BF16)
        h2 = h2 + _dot(act, wd_ref[sl, :])

    hp = _rms(h2, nple_ref[...]).astype(BF16)
    gate = _sigmoid(_dot(hp, wpg_ref[...]))
    h3 = h2 + gate * _dot(p_ref[...].astype(BF16), wple_ref[...])
    out_ref[...] = _rms(h3, nfin_ref[...]) if final_norm else h3


def _post(xf, attn, o_f, o_b, dz, pf, dng, w_out, nffn, w_gate, w_up, w_down, nple, w_pg, w_ple,
          nfin, tm, ff_chunk, final_norm):
    t, d = xf.shape
    row = lambda i: (i, 0)
    const = lambda i: (0, 0)

    def resident(shape):
        return pl.BlockSpec(shape, const, pipeline_mode=pl.Buffered(1))

    ff = w_gate.shape[1]
    return pl.pallas_call(
        functools.partial(_post_kernel, ff_chunk, final_norm),
        grid=(t // tm,),
        in_specs=[
            pl.BlockSpec((tm, d), row),
            pl.BlockSpec((tm, ATTN_Q_W), row),
            pl.BlockSpec((tm, DN_V_W), row),
            pl.BlockSpec((tm, DN_V_W), row),
            pl.BlockSpec((tm, DN_V_W), row),
            pl.BlockSpec((tm, pf.shape[1]), row),
            resident((1, DN_HEAD)),
            resident((ATTN_Q_W + DN_V_W, d)),
            resident((1, d)),
            resident((d, ff)),
            resident((d, ff)),
            resident((ff, d)),
            resident((1, d)),
            resident((d, d)),
            resident((pf.shape[1], d)),
            resident((1, d)),
        ],
        out_specs=pl.BlockSpec((tm, d), row),
        out_shape=jax.ShapeDtypeStruct((t, d), F32),
        compiler_params=pltpu.CompilerParams(
            dimension_semantics=("parallel",), vmem_limit_bytes=VMEM_LIMIT),
        name="post",
    )(xf, attn, o_f, o_b, dz, pf, dng, w_out, nffn, w_gate, w_up, w_down, nple, w_pg, w_ple, nfin)


def _rope_tables(seq):
    pos = jnp.arange(seq, dtype=jnp.int32)
    rowcol = jnp.stack([(pos // GRID_W).astype(F32), (pos % GRID_W).astype(F32)], axis=1)
    sec = ATTN_HEAD_DIM // 2
    inv_freq = ROPE_THETA ** (-jnp.arange(0, sec, 2, dtype=F32) / sec)
    d = np.arange(LANES) % ATTN_HEAD_DIM
    axis = d // sec
    idx = d % (sec // 2)
    sign = np.where((d % sec) < sec // 2, -1.0, 1.0).astype(np.float32)
    ang = rowcol[:, axis] * inv_freq[idx][None, :]
    return jnp.cos(ang), jnp.sin(ang) * sign[None, :]


def kernel(x, p, norm_mix, w_in, conv_w, q_norm, k_norm, a_log, dt_bias, dn_norm, w_out,
           norm_ffn, w_gate, w_up, w_down, norm_ple, w_ple_gate, w_ple, norm_final):
    b, s, d = x.shape
    depth = w_in.shape[0]
    t = b * s
    tm, tq, tk = min(ROW_TILE, s), min(Q_TILE, s), min(KV_TILE, s)
    cos_t, sin_t = _rope_tables(s)

    h = x.reshape(t, d)
    for i in range(depth):
        wi = w_in[i].astype(BF16)
        c = 0
        wq = wi[:, c:c + ATTN_Q_W]; c += ATTN_Q_W
        wk = wi[:, c:c + KV_W]; c += KV_W
        wv = wi[:, c:c + KV_W]; c += KV_W
        wdn = wi[:, c:c + DN_QKV_W]; c += DN_QKV_W
        wz = wi[:, c:c + DN_V_W]; c += DN_V_W
        wba = wi[:, c:]
        wq = wq.reshape(d, ATTN_HEADS, ATTN_HEAD_DIM)
        zeros = jnp.zeros_like(wq)
        lower = (np.arange(ATTN_HEADS) // ATTN_GROUP == 0)[None, :, None]
        wq_exp = jnp.concatenate(
            [jnp.where(lower, wq, zeros), jnp.where(lower, zeros, wq)], axis=-1).reshape(d, Q_EXP_W)
        wba = jnp.pad(wba, ((0, 0), (0, LANES - wba.shape[1])))
        w_cat = jnp.concatenate([wq_exp, wk, wv, wdn, wz, wba], axis=1)

        qg = jnp.tile(q_norm[i], LANES // ATTN_HEAD_DIM)[None, :]
        kg = jnp.tile(k_norm[i], LANES // ATTN_HEAD_DIM)[None, :]
        q, k, va, vb, dqkv, dz, ba = _inproj(h, norm_mix[i][None, :], w_cat, cos_t, sin_t, qg, kg, s, tm)

        score_bound = (ATTN_HEAD_DIM ** 0.5) * jnp.max(jnp.abs(q_norm[i])) * jnp.max(jnp.abs(k_norm[i]))
        bounded = (score_bound <= MAX_UNSHIFTED_SCORE).astype(jnp.int32).reshape(1)
        attn = _attention(bounded, q.reshape(b, s, Q_EXP_W), k.reshape(b, s, KV_W),
                          va.reshape(b, s, KV_W), vb.reshape(b, s, KV_W), tq, tk)

        pad = (2 * DN_HEADS, LANES - 4 * DN_HEADS)
        alog_pad = jnp.pad(a_log[i], pad)[None, :]
        dtb_pad = jnp.pad(dt_bias[i], pad)[None, :]
        dqkv3 = dqkv.reshape(b, s, DN_QKV_W)
        ba3 = ba.reshape(b, s, LANES)
        o_f, o_b = _dn_scan(*_dn_local(dqkv3, ba3, conv_w[i], alog_pad, dtb_pad, jnp.asarray(_dn_masks())))

        h = _post(h, attn.reshape(t, ATTN_Q_W), o_f.reshape(t, DN_V_W), o_b.reshape(t, DN_V_W), dz,
                  p[i].reshape(t, -1), dn_norm[i][None, :], w_out[i].astype(BF16),
                  norm_ffn[i][None, :], w_gate[i].astype(BF16), w_up[i].astype(BF16),
                  w_down[i].astype(BF16), norm_ple[i][None, :], w_ple_gate[i].astype(BF16),
                  w_ple[i].astype(BF16),
                  norm_final[None, :], tm, FF_CHUNK, i == depth - 1)
    return h.reshape(b, s, d)
```

```python
import functools

import numpy as np
import jax
import jax.numpy as jnp
from jax import lax
from jax.experimental import pallas as pl
from jax.experimental.pallas import tpu as pltpu

F32 = jnp.float32
BF16 = jnp.bfloat16

LANES = 128
SUBLANES = 8
VMEM_LIMIT = 60000 * 1024

GRID_W = 64
ATTN_HEADS = 8
ATTN_KV_HEADS = 2
ATTN_HEAD_DIM = 64
ATTN_GROUP = ATTN_HEADS // ATTN_KV_HEADS
ROPE_THETA = 10000.0
DN_HEADS = 4
DN_HEAD = 128
DN_CHUNK = 128
DN_DIRS = 2
DN_LEVELS = 7
N_MASKS = 2 + DN_LEVELS
CONV_W = 5
EPS = 1e-6
LOG2E = float(np.log2(np.e))

KV_W = ATTN_KV_HEADS * ATTN_HEAD_DIM
DN_QKV_W = 3 * DN_HEADS * DN_HEAD
DN_V_W = DN_HEADS * DN_HEAD
ATTN_Q_W = ATTN_HEADS * ATTN_HEAD_DIM

ROW_TILE = 512
Q_TILE = 256
KV_UNROLL = 4
V_ROWS = ATTN_HEAD_DIM + 16
Q_SCALE = ATTN_HEAD_DIM ** -0.5 * LOG2E
MAX_UNSHIFTED_SCORE = 40.0
FF_CHUNK = 256
DN_BLOCK = 256
DN_SUB = LANES
CONV_HALO = SUBLANES


def _dot(a, b):
    return jnp.dot(a, b, preferred_element_type=F32)


def _dot_nt(a, b):
    return lax.dot_general(a, b, (((1,), (1,)), ((), ())), preferred_element_type=F32)


def _rms(x, gain):
    return x * lax.rsqrt(jnp.mean(x * x, axis=-1, keepdims=True) + EPS) * gain


def _sigmoid(x):
    return 1.0 / (1.0 + jnp.exp(-x))


def _inproj_kernel(x_ref, gain_ref, w_ref, cos_ref, sin_ref, qg_ref, kg_ref,
                   q_ref, k_ref, va_ref, vb_ref, dqkv_ref, dz_ref, ba_ref):
    tm = x_ref.shape[0]
    hb = _rms(x_ref[...], gain_ref[...]).astype(BF16)
    cos = cos_ref[...]
    sin = sin_ref[...]
    lane = lax.broadcasted_iota(jnp.int32, (tm, LANES), 1)
    low16 = (lane % 32) < 16

    head0 = lane < ATTN_HEAD_DIM

    def norm_rope(t, gain):
        t2 = t * t
        ss0 = jnp.sum(jnp.where(head0, t2, 0.0), axis=-1, keepdims=True) * (1.0 / ATTN_HEAD_DIM)
        ss1 = jnp.sum(jnp.where(head0, 0.0, t2), axis=-1, keepdims=True) * (1.0 / ATTN_HEAD_DIM)
        tn = t * jnp.where(head0, lax.rsqrt(ss0 + EPS), lax.rsqrt(ss1 + EPS)) * gain
        partner = jnp.where(low16, pltpu.roll(tn, LANES - 16, 1), pltpu.roll(tn, 16, 1))
        return tn * cos + partner * sin

    c0 = 0
    q_all = _dot(hb, w_ref[:, c0:c0 + ATTN_Q_W])
    for c in range(ATTN_Q_W // LANES):
        sl = slice(c * LANES, (c + 1) * LANES)
        q_ref[:, sl] = (norm_rope(q_all[:, sl], qg_ref[...]) * Q_SCALE).astype(BF16)
    c0 += ATTN_Q_W

    kv = _dot(hb, w_ref[:, c0:c0 + 2 * KV_W])
    k_ref[...] = norm_rope(kv[:, :KV_W], kg_ref[...]).astype(BF16)
    v_t = jnp.transpose(kv[:, KV_W:])
    ones = jnp.ones((V_ROWS - ATTN_HEAD_DIM, tm), F32)
    va_ref[0] = jnp.concatenate([v_t[:ATTN_HEAD_DIM], ones], axis=0).astype(BF16)
    vb_ref[0] = jnp.concatenate([v_t[ATTN_HEAD_DIM:], ones], axis=0).astype(BF16)
    c0 += 2 * KV_W

    dqkv_ref[...] = _dot(hb, w_ref[:, c0:c0 + DN_QKV_W]).astype(BF16)
    c0 += DN_QKV_W
    zb = _dot(hb, w_ref[:, c0:c0 + DN_V_W + LANES])
    dz_ref[...] = zb[:, :DN_V_W].astype(BF16)
    ba_ref[...] = zb[:, DN_V_W:]


def _inproj(xf, gain, w_cat, cos_t, sin_t, qg, kg, seq, tm):
    t, d = xf.shape
    nseq = seq // tm
    wcols = w_cat.shape[1]
    row = lambda i: (i, 0)
    const = lambda i: (0, 0)
    tab = lambda i: (i % nseq, 0)
    return pl.pallas_call(
        _inproj_kernel,
        grid=(t // tm,),
        in_specs=[
            pl.BlockSpec((tm, d), row),
            pl.BlockSpec((1, d), const),
            pl.BlockSpec((d, wcols), const),
            pl.BlockSpec((tm, LANES), tab),
            pl.BlockSpec((tm, LANES), tab),
            pl.BlockSpec((1, LANES), const),
            pl.BlockSpec((1, LANES), const),
        ],
        out_specs=[
            pl.BlockSpec((tm, ATTN_Q_W), row),
            pl.BlockSpec((tm, KV_W), row),
            pl.BlockSpec((1, V_ROWS, tm), lambda i: (i, 0, 0)),
            pl.BlockSpec((1, V_ROWS, tm), lambda i: (i, 0, 0)),
            pl.BlockSpec((tm, DN_QKV_W), row),
            pl.BlockSpec((tm, DN_V_W), row),
            pl.BlockSpec((tm, LANES), row),
        ],
        out_shape=[
            jax.ShapeDtypeStruct((t, ATTN_Q_W), BF16),
            jax.ShapeDtypeStruct((t, KV_W), BF16),
            jax.ShapeDtypeStruct((t // tm, V_ROWS, tm), BF16),
            jax.ShapeDtypeStruct((t // tm, V_ROWS, tm), BF16),
            jax.ShapeDtypeStruct((t, DN_QKV_W), BF16),
            jax.ShapeDtypeStruct((t, DN_V_W), BF16),
            jax.ShapeDtypeStruct((t, LANES), F32),
        ],
        compiler_params=pltpu.CompilerParams(
            dimension_semantics=("parallel",), vmem_limit_bytes=VMEM_LIMIT),
        name="inproj",
    )(xf, gain, w_cat, cos_t, sin_t, qg, kg)


def _attn_kernel(bounded_ref, q_ref, k_ref, vat_ref, vbt_ref, o_ref, qt_ref, acc_ref, m_ref):
    tq = q_ref.shape[1]
    nkv = k_ref.shape[1]
    unroll = KV_UNROLL if nkv % KV_UNROLL == 0 else 1
    zeros = jnp.zeros((ATTN_HEAD_DIM, tq), BF16)
    for c in range(ATTN_Q_W // LANES):
        pair_t = jnp.transpose(q_ref[0, :, c * LANES:(c + 1) * LANES].astype(F32)).astype(BF16)
        for e in range(2):
            h = 2 * c + e
            g, j = divmod(h, ATTN_GROUP)
            q_t = pair_t[e * ATTN_HEAD_DIM:(e + 1) * ATTN_HEAD_DIM]
            qt_ref[g, :, j * tq:(j + 1) * tq] = jnp.concatenate(
                [q_t, zeros] if g == 0 else [zeros, q_t], axis=0)
    acc_ref[...] = jnp.zeros(acc_ref.shape, F32)
    vt_refs = (vat_ref, vbt_ref)

    @pl.when(bounded_ref[0] == 1)
    def _():
        def body(i, carry):
            for g in range(ATTN_KV_HEADS):
                upd = None
                for t in range(unroll):
                    tile = i * unroll + t
                    p_t = jnp.exp2(_dot(k_ref[0, tile], qt_ref[g])).astype(BF16)
                    part = _dot(vt_refs[g][0, tile], p_t)
                    upd = part if upd is None else upd + part
                acc_ref[g] += upd
            return carry

        lax.fori_loop(0, nkv // unroll, body, 0)

    @pl.when(bounded_ref[0] == 0)
    def _():
        m_ref[...] = jnp.full(m_ref.shape, -jnp.inf, F32)

        def body(i, carry):
            kt = k_ref[0, i]
            for g in range(ATTN_KV_HEADS):
                s_t = _dot(kt, qt_ref[g])
                m_prev = m_ref[g]
                m_new = jnp.maximum(m_prev, jnp.max(s_t, axis=0, keepdims=True))
                p_t = jnp.exp2(s_t - m_new[0:1, :]).astype(BF16)
                alpha = jnp.exp2(m_prev - m_new)[0:1, :]
                acc_ref[g] = alpha * acc_ref[g] + _dot(vt_refs[g][0, i], p_t)
                m_ref[g] = m_new
            return carry

        lax.fori_loop(0, nkv, body, 0)

    for g in range(ATTN_KV_HEADS):
        a = acc_ref[g]
        o_t = a[:ATTN_HEAD_DIM] / a[ATTN_HEAD_DIM:ATTN_HEAD_DIM + 1]
        for jp in range(ATTN_GROUP // 2):
            pair = jnp.concatenate(
                [o_t[:, (2 * jp) * tq:(2 * jp + 1) * tq], o_t[:, (2 * jp + 1) * tq:(2 * jp + 2) * tq]],
                axis=0)
            c = g * (ATTN_GROUP // 2) + jp
            o_ref[0, :, c * LANES:(c + 1) * LANES] = jnp.transpose(pair).astype(BF16)


def _attention(bounded, q, k, vat, vbt, tq):
    b, s, _ = q.shape
    _, nkv, tk, _ = k.shape
    cols = ATTN_GROUP * tq
    return pl.pallas_call(
        _attn_kernel,
        grid_spec=pltpu.PrefetchScalarGridSpec(
            num_scalar_prefetch=1,
            grid=(b, s // tq),
            in_specs=[
                pl.BlockSpec((1, tq, ATTN_Q_W), lambda bi, qi, flag: (bi, qi, 0)),
                pl.BlockSpec((1, nkv, tk, KV_W), lambda bi, qi, flag: (bi, 0, 0, 0)),
                pl.BlockSpec((1, nkv, V_ROWS, tk), lambda bi, qi, flag: (bi, 0, 0, 0)),
                pl.BlockSpec((1, nkv, V_ROWS, tk), lambda bi, qi, flag: (bi, 0, 0, 0)),
            ],
            out_specs=pl.BlockSpec((1, tq, ATTN_Q_W), lambda bi, qi, flag: (bi, qi, 0)),
            scratch_shapes=[
                pltpu.VMEM((ATTN_KV_HEADS, LANES, cols), BF16),
                pltpu.VMEM((ATTN_KV_HEADS, V_ROWS, cols), F32),
                pltpu.VMEM((ATTN_KV_HEADS, SUBLANES, cols), F32),
            ],
        ),
        out_shape=jax.ShapeDtypeStruct((b, s, ATTN_Q_W), BF16),
        compiler_params=pltpu.CompilerParams(
            dimension_semantics=("parallel", "arbitrary"), vmem_limit_bytes=VMEM_LIMIT),
        name="attention",
    )(bounded, q, k, vat, vbt)


def _dn_masks():
    n = DN_SUB
    row = np.arange(n)[:, None]
    col = np.arange(n)[None, :]
    same = (row // DN_CHUNK) == (col // DN_CHUNK)
    out = []
    for rev in (False, True):
        a, b = (col, row) if rev else (row, col)
        out.append(same & (a >= b))
        out.append(same & (a > b))
        for lb in range(DN_LEVELS):
            out.append(((a >> lb) == (b >> lb) + 1) & (((a >> lb) & 1) == 1))
    return np.stack(out).astype(np.float32)


def _dn_local_kernel(qkv_ref, prev_ref, next_ref, ba_ref, convw_ref, alog_ref, dtb_ref, mask_ref,
                     wq_ref, u_ref, ot_ref, kdt_ref, gt_ref):
    n = DN_BLOCK
    nch = n // DN_CHUNK
    blk = pl.program_id(1)
    nblk = pl.num_programs(1)

    top = jnp.where(blk == 0, 0.0, prev_ref[0].astype(F32))
    bot = jnp.where(blk == nblk - 1, 0.0, next_ref[0].astype(F32))
    ext = jnp.concatenate([top, qkv_ref[0].astype(F32), bot], axis=0)
    rows = n + 2 * CONV_HALO
    acc = None
    for j in range(CONV_W):
        shift = (CONV_W // 2 - j) % rows
        xs = ext if shift == 0 else pltpu.roll(ext, shift, 0)
        term = xs[CONV_HALO:CONV_HALO + n] * convw_ref[j:j + 1, :]
        acc = term if acc is None else acc + term
    y = acc * _sigmoid(acc)

    ba = ba_ref[0]
    beta_all = _sigmoid(ba)
    sp_arg = ba + dtb_ref[...]
    softplus = jnp.maximum(sp_arg, 0.0) + jnp.log(1.0 + jnp.exp(-jnp.abs(sp_arg)))
    g2 = (-LOG2E) * jnp.exp(alog_ref[...]) * softplus

    in_chunk = lax.broadcasted_iota(jnp.int32, (n, LANES), 0) % DN_CHUNK
    g_fwd = g2
    shift = 1
    while shift < DN_CHUNK:
        g_fwd = g_fwd + jnp.where(in_chunk >= shift, pltpu.roll(g_fwd, shift, 0), 0.0)
        shift *= 2
    total = jnp.concatenate(
        [jnp.broadcast_to(g_fwd[(c + 1) * DN_CHUNK - 1:(c + 1) * DN_CHUNK, :], (DN_CHUNK, LANES))
         for c in range(nch)], axis=0)
    g_bwd = total - g_fwd + g2
    g_cols = (g_fwd, g_bwd)
    g_rows = (jnp.transpose(g_fwd), jnp.transpose(g_bwd))

    m = DN_SUB
    row = lax.broadcasted_iota(jnp.int32, (m, m), 0)
    col = lax.broadcasted_iota(jnp.int32, (m, m), 1)
    eye = jnp.where(row == col, 1.0, 0.0)
    subs = [slice(s * m, (s + 1) * m) for s in range(n // m)]

    qs, ks, vs, kk, qk_raw = [], [], [], [], []
    for h in range(DN_HEADS):
        q = y[:, h * DN_HEAD:(h + 1) * DN_HEAD]
        k = y[:, (DN_HEADS + h) * DN_HEAD:(DN_HEADS + h + 1) * DN_HEAD]
        q = q * lax.rsqrt(jnp.sum(q * q, axis=-1, keepdims=True) + EPS) * (DN_HEAD ** -0.5)
        k = k * lax.rsqrt(jnp.sum(k * k, axis=-1, keepdims=True) + EPS)
        k_bf = k.astype(BF16)
        q_bf = q.astype(BF16)
        qs.append(q)
        ks.append(k)
        vs.append(y[:, (2 * DN_HEADS + h) * DN_HEAD:(2 * DN_HEADS + h + 1) * DN_HEAD])
        kk.append([_dot_nt(k_bf[r], k_bf[r]) for r in subs])
        qk_raw.append([_dot_nt(q_bf[r], k_bf[r]) for r in subs])

    hds = [(d, h) for d in range(DN_DIRS) for h in range(DN_HEADS)]
    prob = [(i, s) for i in range(len(hds)) for s in range(len(subs))]
    a_bf, x_inv, qk_bf, rhs_bf, qd, g_tot = {}, {}, {}, [], [], []
    for i, (d, h) in enumerate(hds):
        lane = 2 * DN_HEADS + i
        gc = jnp.broadcast_to(g_cols[d][:, lane:lane + 1], (n, LANES))
        beta = jnp.broadcast_to(beta_all[:, i:i + 1], (n, LANES))
        for s, r in enumerate(subs):
            g_row = g_rows[d][lane:lane + 1, r]
            decay = jnp.exp2(jnp.minimum(gc[r] - g_row, 0.0))
            a_mat = kk[h][s] * beta[r] * (decay * mask_ref[d * N_MASKS + 1])
            a_bf[i, s] = a_mat.astype(BF16)
            x_inv[i, s] = eye - a_mat * mask_ref[d * N_MASKS + 2]
            qk_bf[i, s] = (qk_raw[h][s] * (decay * mask_ref[d * N_MASKS])).astype(BF16)
        e_gc = jnp.exp2(gc)
        rhs_bf.append(jnp.concatenate([vs[h] * beta, ks[h] * beta * e_gc], axis=1).astype(BF16))
        qd.append(qs[h] * e_gc)

        edge = 0 if d else DN_CHUNK - 1
        gl_rows = [gc[c * DN_CHUNK + edge:c * DN_CHUNK + edge + 1, :] for c in range(nch)]
        gl = jnp.concatenate([jnp.broadcast_to(t, (DN_CHUNK, LANES)) for t in gl_rows], axis=0)
        g_tot.append([jnp.exp2(t) for t in gl_rows])
        kdt_ref[0, i] = jnp.transpose(ks[h] * jnp.exp2(gl - gc)).astype(BF16)

    for lb in range(1, DN_LEVELS):
        for i, s in prob:
            x_bf = x_inv[i, s].astype(BF16)
            upd = _dot(_dot(x_bf, a_bf[i, s]).astype(BF16), x_bf)
            x_inv[i, s] = x_inv[i, s] - upd * mask_ref[hds[i][0] * N_MASKS + 2 + lb]

    sols = {(i, s): _dot(x_inv[i, s].astype(BF16), rhs_bf[i][subs[s]]) for i, s in prob}
    corrs = {(i, s): _dot(qk_bf[i, s], sols[i, s].astype(BF16)) for i, s in prob}
    for i in range(len(hds)):
        sol = jnp.concatenate([sols[i, s] for s in range(len(subs))], axis=0)
        corr = jnp.concatenate([corrs[i, s] for s in range(len(subs))], axis=0)
        w_bf = sol[:, DN_HEAD:].astype(BF16)
        qe_bf = (qd[i] - corr[:, DN_HEAD:]).astype(BF16)
        pieces = []
        for c in range(nch):
            sl = slice(c * DN_CHUNK, (c + 1) * DN_CHUNK)
            pieces += [w_bf[sl], qe_bf[sl]]
        wq_ref[0, i] = jnp.concatenate(pieces, axis=0)
        u_ref[0, i] = sol[:, :DN_HEAD].astype(BF16)
        ot_ref[0, i] = corr[:, :DN_HEAD].astype(BF16)

    for c in range(nch):
        gt_ref[0, c] = jnp.concatenate([t[c] for t in g_tot], axis=0)


def _dn_local(dqkv, ba, conv_w, alog_pad, dtb_pad, masks):
    b, s, _ = dqkv.shape
    n = DN_BLOCK
    nblk = s // n
    per = n // CONV_HALO
    nhalo = s // CONV_HALO
    hds = DN_DIRS * DN_HEADS
    return pl.pallas_call(
        _dn_local_kernel,
        grid=(b, nblk),
        in_specs=[
            pl.BlockSpec((1, n, DN_QKV_W), lambda bi, c: (bi, c, 0)),
            pl.BlockSpec((1, CONV_HALO, DN_QKV_W), lambda bi, c: (bi, jnp.maximum(c * per - 1, 0), 0)),
            pl.BlockSpec((1, CONV_HALO, DN_QKV_W),
                         lambda bi, c: (bi, jnp.minimum((c + 1) * per, nhalo - 1), 0)),
            pl.BlockSpec((1, n, LANES), lambda bi, c: (bi, c, 0)),
            pl.BlockSpec((CONV_W, DN_QKV_W), lambda bi, c: (0, 0)),
            pl.BlockSpec((1, LANES), lambda bi, c: (0, 0)),
            pl.BlockSpec((1, LANES), lambda bi, c: (0, 0)),
            pl.BlockSpec(masks.shape, lambda bi, c: (0, 0, 0)),
        ],
        out_specs=[
            pl.BlockSpec((1, hds, 2 * n, DN_HEAD), lambda bi, c: (bi, 0, c, 0)),
            pl.BlockSpec((1, hds, n, DN_HEAD), lambda bi, c: (bi, 0, c, 0)),
            pl.BlockSpec((1, hds, n, DN_HEAD), lambda bi, c: (bi, 0, c, 0)),
            pl.BlockSpec((1, hds, DN_HEAD, n), lambda bi, c: (bi, 0, 0, c)),
            pl.BlockSpec((1, n // DN_CHUNK, hds, LANES), lambda bi, c: (bi, c, 0, 0)),
        ],
        out_shape=[
            jax.ShapeDtypeStruct((b, hds, 2 * s, DN_HEAD), BF16),
            jax.ShapeDtypeStruct((b, hds, s, DN_HEAD), BF16),
            jax.ShapeDtypeStruct((b, hds, s, DN_HEAD), BF16),
            jax.ShapeDtypeStruct((b, hds, DN_HEAD, s), BF16),
            jax.ShapeDtypeStruct((b, s // DN_CHUNK, hds, LANES), F32),
        ],
        compiler_params=pltpu.CompilerParams(
            dimension_semantics=("parallel", "parallel"), vmem_limit_bytes=VMEM_LIMIT),
        name="deltanet_local",
    )(dqkv, dqkv, dqkv, ba, conv_w, alog_pad, dtb_pad, masks)


def _dn_scan_kernel(wqf_ref, wqb_ref, uf_ref, ub_ref, otf_ref, otb_ref, kdf_ref, kdb_ref,
                    gtf_ref, gtb_ref, of_ref, ob_ref, s_ref):
    n = DN_BLOCK
    nch = n // DN_CHUNK

    @pl.when(pl.program_id(1) == 0)
    def _():
        s_ref[...] = jnp.zeros(s_ref.shape, F32)

    refs = ((wqf_ref, uf_ref, otf_ref, kdf_ref, gtf_ref), (wqb_ref, ub_ref, otb_ref, kdb_ref, gtb_ref))
    hds = DN_DIRS * DN_HEADS
    states = [s_ref[hd] for hd in range(hds)]
    outs = [[None] * nch for _ in range(hds)]
    for r in range(nch):
        for d in range(DN_DIRS):
            wq_ref, u_ref, ot_ref, kd_ref, gt_ref = refs[d]
            c = nch - 1 - r if d else r
            rows = slice(c * DN_CHUNK, (c + 1) * DN_CHUNK)
            for h in range(DN_HEADS):
                hd = d * DN_HEADS + h
                s_bf = states[hd].astype(BF16)
                res = _dot(wq_ref[0, h, 2 * c * DN_CHUNK:2 * (c + 1) * DN_CHUNK, :], s_bf)
                vn = u_ref[0, h, rows, :].astype(F32) - res[:DN_CHUNK]
                outs[hd][c] = res[DN_CHUNK:] + ot_ref[0, h, rows, :].astype(F32)
                pieces = []
                if c:
                    pieces.append(jnp.zeros((c * DN_CHUNK, LANES), BF16))
                pieces.append(vn.astype(BF16))
                if nch - 1 - c:
                    pieces.append(jnp.zeros(((nch - 1 - c) * DN_CHUNK, LANES), BF16))
                vn_pad = jnp.concatenate(pieces, axis=0)
                states[hd] = states[hd] * gt_ref[0, c, hd:hd + 1, :] + _dot(kd_ref[0, h], vn_pad)
    for hd in range(hds):
        s_ref[hd] = states[hd]
    for d, o_ref in enumerate((of_ref, ob_ref)):
        for h in range(DN_HEADS):
            o_ref[0, :, h * DN_HEAD:(h + 1) * DN_HEAD] = jnp.concatenate(
                outs[d * DN_HEADS + h], axis=0).astype(BF16)


def _dn_scan(wq, u, ot, kdt, gt):
    b, hds, s, _ = u.shape
    n = DN_BLOCK
    nblk = s // n
    nch = n // DN_CHUNK
    fwd = lambda bi, c: c
    bwd = lambda bi, c: nblk - 1 - c

    def pair(shape, index):
        return [pl.BlockSpec(shape, functools.partial(index, 0, fwd)),
                pl.BlockSpec(shape, functools.partial(index, 1, bwd))]

    rows = lambda d, pos, bi, c: (bi, d, pos(bi, c), 0)
    lanes = lambda d, pos, bi, c: (bi, d, 0, pos(bi, c))
    gts = lambda d, pos, bi, c: (bi, pos(bi, c), 0, 0)
    return pl.pallas_call(
        _dn_scan_kernel,
        grid=(b, nblk),
        in_specs=(pair((1, DN_HEADS, 2 * n, DN_HEAD), rows) + pair((1, DN_HEADS, n, DN_HEAD), rows)
                  + pair((1, DN_HEADS, n, DN_HEAD), rows) + pair((1, DN_HEADS, DN_HEAD, n), lanes)
                  + pair((1, nch, hds, LANES), gts)),
        out_specs=[
            pl.BlockSpec((1, n, DN_V_W), lambda bi, c: (bi, c, 0)),
            pl.BlockSpec((1, n, DN_V_W), lambda bi, c: (bi, nblk - 1 - c, 0)),
        ],
        out_shape=[jax.ShapeDtypeStruct((b, s, DN_V_W), BF16)] * 2,
        scratch_shapes=[pltpu.VMEM((hds, DN_HEAD, DN_HEAD), F32)],
        compiler_params=pltpu.CompilerParams(
            dimension_semantics=("parallel", "arbitrary"), vmem_limit_bytes=VMEM_LIMIT),
        name="deltanet_scan",
    )(wq, wq, u, u, ot, ot, kdt, kdt, gt, gt)


def _post_kernel(ff_chunk, final_norm, x_ref, attn_ref, of_ref, ob_ref, dz_ref, p_ref, dng_ref, wout_ref,
                 nffn_ref, wg_ref, wu_ref, wd_ref, nple_ref, wpg_ref, wple_ref, nfin_ref, out_ref):
    o = of_ref[...].astype(F32) + ob_ref[...].astype(F32)
    z = dz_ref[...].astype(F32)
    parts = [attn_ref[...]]
    for h in range(DN_HEADS):
        sl = slice(h * DN_HEAD, (h + 1) * DN_HEAD)
        zh = z[:, sl]
        parts.append((_rms(o[:, sl], dng_ref[...]) * (zh * _sigmoid(zh))).astype(BF16))
    mixed = jnp.concatenate(parts, axis=1)
    h1 = x_ref[...] + _dot(mixed, wout_ref[...])

    hn = _rms(h1, nffn_ref[...]).astype(BF16)
    h2 = h1
    ff = wg_ref.shape[1]
    for c in range(ff // ff_chunk):
        sl = slice(c * ff_chunk, (c + 1) * ff_chunk)
        gate = _dot(hn, wg_ref[:, sl])
        up = _dot(hn, wu_ref[:, sl])
        act = (gate * _sigmoid(gate) * up).astype(BF16)
        h2 = h2 + _dot(act, wd_ref[sl, :])

    hp = _rms(h2, nple_ref[...]).astype(BF16)
    gate = _sigmoid(_dot(hp, wpg_ref[...]))
    h3 = h2 + gate * _dot(p_ref[...].astype(BF16), wple_ref[...])
    out_ref[...] = _rms(h3, nfin_ref[...]) if final_norm else h3


def _post(xf, attn, o_f, o_b, dz, pf, dng, w_out, nffn, w_gate, w_up, w_down, nple, w_pg, w_ple,
          nfin, tm, ff_chunk, final_norm):
    t, d = xf.shape
    row = lambda i: (i, 0)
    const = lambda i: (0, 0)

    def resident(shape):
        return pl.BlockSpec(shape, const, pipeline_mode=pl.Buffered(1))

    ff = w_gate.shape[1]
    return pl.pallas_call(
        functools.partial(_post_kernel, ff_chunk, final_norm),
        grid=(t // tm,),
        in_specs=[
            pl.BlockSpec((tm, d), row),
            pl.BlockSpec((tm, ATTN_Q_W), row),
            pl.BlockSpec((tm, DN_V_W), row),
            pl.BlockSpec((tm, DN_V_W), row),
            pl.BlockSpec((tm, DN_V_W), row),
            pl.BlockSpec((tm, pf.shape[1]), row),
            resident((1, DN_HEAD)),
            resident((ATTN_Q_W + DN_V_W, d)),
            resident((1, d)),
            resident((d, ff)),
            resident((d, ff)),
            resident((ff, d)),
            resident((1, d)),
            resident((d, d)),
            resident((pf.shape[1], d)),
            resident((1, d)),
        ],
        out_specs=pl.BlockSpec((tm, d), row),
        out_shape=jax.ShapeDtypeStruct((t, d), F32),
        compiler_params=pltpu.CompilerParams(
            dimension_semantics=("parallel",), vmem_limit_bytes=VMEM_LIMIT),
        name="post",
    )(xf, attn, o_f, o_b, dz, pf, dng, w_out, nffn, w_gate, w_up, w_down, nple, w_pg, w_ple, nfin)


def _rope_tables(seq):
    pos = jnp.arange(seq, dtype=jnp.int32)
    rowcol = jnp.stack([(pos // GRID_W).astype(F32), (pos % GRID_W).astype(F32)], axis=1)
    sec = ATTN_HEAD_DIM // 2
    inv_freq = ROPE_THETA ** (-jnp.arange(0, sec, 2, dtype=F32) / sec)
    d = np.arange(LANES) % ATTN_HEAD_DIM
    axis = d // sec
    idx = d % (sec // 2)
    sign = np.where((d % sec) < sec // 2, -1.0, 1.0).astype(np.float32)
    ang = rowcol[:, axis] * inv_freq[idx][None, :]
    return jnp.cos(ang), jnp.sin(ang) * sign[None, :]


def kernel(x, p, norm_mix, w_in, conv_w, q_norm, k_norm, a_log, dt_bias, dn_norm, w_out,
           norm_ffn, w_gate, w_up, w_down, norm_ple, w_ple_gate, w_ple, norm_final):
    b, s, d = x.shape
    depth = w_in.shape[0]
    t = b * s
    tm, tq = min(ROW_TILE, s), min(Q_TILE, s)
    cos_t, sin_t = _rope_tables(s)

    h = x.reshape(t, d)
    for i in range(depth):
        wi = w_in[i].astype(BF16)
        w_cat = jnp.pad(wi, ((0, 0), (0, LANES - 4 * DN_HEADS)))

        qg = jnp.tile(q_norm[i], LANES // ATTN_HEAD_DIM)[None, :]
        kg = jnp.tile(k_norm[i], LANES // ATTN_HEAD_DIM)[None, :]
        q, k, va, vb, dqkv, dz, ba = _inproj(h, norm_mix[i][None, :], w_cat, cos_t, sin_t, qg, kg, s, tm)

        score_bound = (ATTN_HEAD_DIM ** 0.5) * jnp.max(jnp.abs(q_norm[i])) * jnp.max(jnp.abs(k_norm[i]))
        bounded = (score_bound <= MAX_UNSHIFTED_SCORE).astype(jnp.int32).reshape(1)
        attn = _attention(bounded, q.reshape(b, s, ATTN_Q_W), k.reshape(b, s // tm, tm, KV_W),
                          va.reshape(b, s // tm, V_ROWS, tm), vb.reshape(b, s // tm, V_ROWS, tm), tq)

        pad = (2 * DN_HEADS, LANES - 4 * DN_HEADS)
        alog_pad = jnp.pad(a_log[i], pad)[None, :]
        dtb_pad = jnp.pad(dt_bias[i], pad)[None, :]
        dqkv3 = dqkv.reshape(b, s, DN_QKV_W)
        ba3 = ba.reshape(b, s, LANES)
        o_f, o_b = _dn_scan(*_dn_local(dqkv3, ba3, conv_w[i], alog_pad, dtb_pad, jnp.asarray(_dn_masks())))

        h = _post(h, attn.reshape(t, ATTN_Q_W), o_f.reshape(t, DN_V_W), o_b.reshape(t, DN_V_W), dz,
                  p[i].reshape(t, -1), dn_norm[i][None, :], w_out[i].astype(BF16),
                  norm_ffn[i][None, :], w_gate[i].astype(BF16), w_up[i].astype(BF16),
                  w_down[i].astype(BF16), norm_ple[i][None, :], w_ple_gate[i].astype(BF16),
                  w_ple[i].astype(BF16),
                  norm_final[None, :], tm, FF_CHUNK, i == depth - 1)
    return h.reshape(b, s, d)
```

```python
import functools

import numpy as np
import jax
import jax.numpy as jnp
from jax import lax
from jax.experimental import pallas as pl
from jax.experimental.pallas import tpu as pltpu

F32 = jnp.float32
BF16 = jnp.bfloat16

LANES = 128
SUBLANES = 8
VMEM_LIMIT = 60000 * 1024

GRID_W = 64
ATTN_HEADS = 8
ATTN_KV_HEADS = 2
ATTN_HEAD_DIM = 64
ATTN_GROUP = ATTN_HEADS // ATTN_KV_HEADS
ROPE_THETA = 10000.0
DN_HEADS = 4
DN_HEAD = 128
DN_CHUNK = 128
DN_DIRS = 2
DN_LEVELS = 7
N_MASKS = 2 + DN_LEVELS
CONV_W = 5
EPS = 1e-6
LOG2E = float(np.log2(np.e))

KV_W = ATTN_KV_HEADS * ATTN_HEAD_DIM
DN_QKV_W = 3 * DN_HEADS * DN_HEAD
DN_V_W = DN_HEADS * DN_HEAD
ATTN_Q_W = ATTN_HEADS * ATTN_HEAD_DIM

ROW_TILE = 512
Q_TILE = 512
KV_UNROLL = 4
V_ROWS = ATTN_HEAD_DIM + 16
Q_SCALE = ATTN_HEAD_DIM ** -0.5 * LOG2E
MAX_UNSHIFTED_SCORE = 40.0
FF_CHUNK = 256
DN_BLOCK = 256
DN_SUB = LANES
CONV_HALO = SUBLANES


def _dot(a, b):
    return jnp.dot(a, b, preferred_element_type=F32)


def _dot_nt(a, b):
    return lax.dot_general(a, b, (((1,), (1,)), ((), ())), preferred_element_type=F32)


def _rms(x, gain):
    return x * lax.rsqrt(jnp.mean(x * x, axis=-1, keepdims=True) + EPS) * gain


def _sigmoid(x):
    return 1.0 / (1.0 + jnp.exp(-x))


def _inproj_kernel(x_ref, gain_ref, w_ref, cos_ref, sin_ref, qg_ref, kg_ref,
                   q_ref, k_ref, va_ref, vb_ref, dqkv_ref, dz_ref, ba_ref):
    tm = x_ref.shape[0]
    hb = _rms(x_ref[...], gain_ref[...]).astype(BF16)
    cos = cos_ref[...]
    sin = sin_ref[...]
    lane = lax.broadcasted_iota(jnp.int32, (tm, LANES), 1)
    low16 = (lane % 32) < 16
    head0 = lane < ATTN_HEAD_DIM

    def norm_rope(t, gain):
        t2 = t * t
        ss0 = jnp.sum(jnp.where(head0, t2, 0.0), axis=-1, keepdims=True) * (1.0 / ATTN_HEAD_DIM)
        ss1 = jnp.sum(jnp.where(head0, 0.0, t2), axis=-1, keepdims=True) * (1.0 / ATTN_HEAD_DIM)
        tn = t * jnp.where(head0, lax.rsqrt(ss0 + EPS), lax.rsqrt(ss1 + EPS)) * gain
        partner = jnp.where(low16, pltpu.roll(tn, LANES - 16, 1), pltpu.roll(tn, 16, 1))
        return tn * cos + partner * sin

    c0 = 0
    q_all = _dot(hb, w_ref[:, c0:c0 + ATTN_Q_W])
    for c in range(ATTN_Q_W // LANES):
        sl = slice(c * LANES, (c + 1) * LANES)
        q_ref[:, sl] = (norm_rope(q_all[:, sl], qg_ref[...]) * Q_SCALE).astype(BF16)
    c0 += ATTN_Q_W

    kv = _dot(hb, w_ref[:, c0:c0 + 2 * KV_W])
    k_ref[...] = norm_rope(kv[:, :KV_W], kg_ref[...]).astype(BF16)
    v_t = jnp.transpose(kv[:, KV_W:])
    ones = jnp.ones((V_ROWS - ATTN_HEAD_DIM, tm), F32)
    va_ref[0] = jnp.concatenate([v_t[:ATTN_HEAD_DIM], ones], axis=0).astype(BF16)
    vb_ref[0] = jnp.concatenate([v_t[ATTN_HEAD_DIM:], ones], axis=0).astype(BF16)
    c0 += 2 * KV_W

    dqkv_ref[...] = _dot(hb, w_ref[:, c0:c0 + DN_QKV_W]).astype(BF16)
    c0 += DN_QKV_W
    zb = _dot(hb, w_ref[:, c0:c0 + DN_V_W + LANES])
    dz_ref[...] = zb[:, :DN_V_W].astype(BF16)
    ba_ref[...] = zb[:, DN_V_W:]


def _inproj(xf, gain, w_cat, cos_t, sin_t, qg, kg, seq, tm):
    t, d = xf.shape
    nseq = seq // tm
    wcols = w_cat.shape[1]
    row = lambda i: (i, 0)
    const = lambda i: (0, 0)
    tab = lambda i: (i % nseq, 0)
    return pl.pallas_call(
        _inproj_kernel,
        grid=(t // tm,),
        in_specs=[
            pl.BlockSpec((tm, d), row),
            pl.BlockSpec((1, d), const),
            pl.BlockSpec((d, wcols), const),
            pl.BlockSpec((tm, LANES), tab),
            pl.BlockSpec((tm, LANES), tab),
            pl.BlockSpec((1, LANES), const),
            pl.BlockSpec((1, LANES), const),
        ],
        out_specs=[
            pl.BlockSpec((tm, ATTN_Q_W), row),
            pl.BlockSpec((tm, KV_W), row),
            pl.BlockSpec((1, V_ROWS, tm), lambda i: (i, 0, 0)),
            pl.BlockSpec((1, V_ROWS, tm), lambda i: (i, 0, 0)),
            pl.BlockSpec((tm, DN_QKV_W), row),
            pl.BlockSpec((tm, DN_V_W), row),
            pl.BlockSpec((tm, LANES), row),
        ],
        out_shape=[
            jax.ShapeDtypeStruct((t, ATTN_Q_W), BF16),
            jax.ShapeDtypeStruct((t, KV_W), BF16),
            jax.ShapeDtypeStruct((t // tm, V_ROWS, tm), BF16),
            jax.ShapeDtypeStruct((t // tm, V_ROWS, tm), BF16),
            jax.ShapeDtypeStruct((t, DN_QKV_W), BF16),
            jax.ShapeDtypeStruct((t, DN_V_W), BF16),
            jax.ShapeDtypeStruct((t, LANES), F32),
        ],
        compiler_params=pltpu.CompilerParams(
            dimension_semantics=("parallel",), vmem_limit_bytes=VMEM_LIMIT),
        name="inproj",
    )(xf, gain, w_cat, cos_t, sin_t, qg, kg)


def _attn_kernel(bounded_ref, q_ref, k_ref, vat_ref, vbt_ref, o_ref, qt_ref, acc_ref, m_ref):
    tq = q_ref.shape[1]
    nkv = k_ref.shape[1]
    unroll = KV_UNROLL if nkv % KV_UNROLL == 0 else 1
    zeros = jnp.zeros((ATTN_HEAD_DIM, tq), BF16)
    for c in range(ATTN_Q_W // LANES):
        pair_t = jnp.transpose(q_ref[0, :, c * LANES:(c + 1) * LANES].astype(F32)).astype(BF16)
        for e in range(2):
            h = 2 * c + e
            g, j = divmod(h, ATTN_GROUP)
            q_t = pair_t[e * ATTN_HEAD_DIM:(e + 1) * ATTN_HEAD_DIM]
            qt_ref[g, :, j * tq:(j + 1) * tq] = jnp.concatenate(
                [q_t, zeros] if g == 0 else [zeros, q_t], axis=0)
    acc_ref[...] = jnp.zeros(acc_ref.shape, F32)
    vt_refs = (vat_ref, vbt_ref)

    @pl.when(bounded_ref[0] == 1)
    def _():
        def body(i, carry):
            for g in range(ATTN_KV_HEADS):
                upd = None
                for t in range(unroll):
                    tile = i * unroll + t
                    p_t = jnp.exp2(_dot(k_ref[0, tile], qt_ref[g])).astype(BF16)
                    part = _dot(vt_refs[g][0, tile], p_t)
                    upd = part if upd is None else upd + part
                acc_ref[g] += upd
            return carry

        lax.fori_loop(0, nkv // unroll, body, 0)

    @pl.when(bounded_ref[0] == 0)
    def _():
        m_ref[...] = jnp.full(m_ref.shape, -jnp.inf, F32)

        def body(i, carry):
            kt = k_ref[0, i]
            for g in range(ATTN_KV_HEADS):
                s_t = _dot(kt, qt_ref[g])
                m_prev = m_ref[g]
                m_new = jnp.maximum(m_prev, jnp.max(s_t, axis=0, keepdims=True))
                p_t = jnp.exp2(s_t - m_new[0:1, :]).astype(BF16)
                alpha = jnp.exp2(m_prev - m_new)[0:1, :]
                acc_ref[g] = alpha * acc_ref[g] + _dot(vt_refs[g][0, i], p_t)
                m_ref[g] = m_new
            return carry

        lax.fori_loop(0, nkv, body, 0)

    for g in range(ATTN_KV_HEADS):
        a = acc_ref[g]
        o_t = a[:ATTN_HEAD_DIM] / a[ATTN_HEAD_DIM:ATTN_HEAD_DIM + 1]
        for jp in range(ATTN_GROUP // 2):
            pair = jnp.concatenate(
                [o_t[:, (2 * jp) * tq:(2 * jp + 1) * tq], o_t[:, (2 * jp + 1) * tq:(2 * jp + 2) * tq]],
                axis=0)
            c = g * (ATTN_GROUP // 2) + jp
            o_ref[0, :, c * LANES:(c + 1) * LANES] = jnp.transpose(pair).astype(BF16)


def _attention(bounded, q, k, vat, vbt, tq):
    b, s, _ = q.shape
    _, nkv, tk, _ = k.shape
    cols = ATTN_GROUP * tq
    return pl.pallas_call(
        _attn_kernel,
        grid_spec=pltpu.PrefetchScalarGridSpec(
            num_scalar_prefetch=1,
            grid=(b, s // tq),
            in_specs=[
                pl.BlockSpec((1, tq, ATTN_Q_W), lambda bi, qi, flag: (bi, qi, 0)),
                pl.BlockSpec((1, nkv, tk, KV_W), lambda bi, qi, flag: (bi, 0, 0, 0)),
                pl.BlockSpec((1, nkv, V_ROWS, tk), lambda bi, qi, flag: (bi, 0, 0, 0)),
                pl.BlockSpec((1, nkv, V_ROWS, tk), lambda bi, qi, flag: (bi, 0, 0, 0)),
            ],
            out_specs=pl.BlockSpec((1, tq, ATTN_Q_W), lambda bi, qi, flag: (bi, qi, 0)),
            scratch_shapes=[
                pltpu.VMEM((ATTN_KV_HEADS, LANES, cols), BF16),
                pltpu.VMEM((ATTN_KV_HEADS, V_ROWS, cols), F32),
                pltpu.VMEM((ATTN_KV_HEADS, SUBLANES, cols), F32),
            ],
        ),
        out_shape=jax.ShapeDtypeStruct((b, s, ATTN_Q_W), BF16),
        compiler_params=pltpu.CompilerParams(
            dimension_semantics=("parallel", "arbitrary"), vmem_limit_bytes=VMEM_LIMIT),
        name="attention",
    )(bounded, q, k, vat, vbt)


def _dn_masks():
    n = DN_SUB
    row = np.arange(n)[:, None]
    col = np.arange(n)[None, :]
    same = (row // DN_CHUNK) == (col // DN_CHUNK)
    out = []
    for rev in (False, True):
        a, b = (col, row) if rev else (row, col)
        out.append(same & (a >= b))
        out.append(same & (a > b))
        for lb in range(DN_LEVELS):
            out.append(((a >> lb) == (b >> lb) + 1) & (((a >> lb) & 1) == 1))
    return np.stack(out).astype(np.float32)


def _dn_local_kernel(qkv_ref, prev_ref, next_ref, ba_ref, convw_ref, alog_ref, dtb_ref, mask_ref,
                     wq_ref, u_ref, ot_ref, kdt_ref, gt_ref):
    n = DN_BLOCK
    nch = n // DN_CHUNK
    blk = pl.program_id(1)
    nblk = pl.num_programs(1)

    rows = n + 2 * CONV_HALO
    cols = []
    for c in range(DN_QKV_W // LANES):
        lanes = slice(c * LANES, (c + 1) * LANES)
        top = jnp.where(blk == 0, 0.0, prev_ref[0, :, lanes].astype(F32))
        bot = jnp.where(blk == nblk - 1, 0.0, next_ref[0, :, lanes].astype(F32))
        ext = jnp.concatenate([top, qkv_ref[0, :, lanes].astype(F32), bot], axis=0)
        acc = None
        for j in range(CONV_W):
            shift = (CONV_W // 2 - j) % rows
            xs = ext if shift == 0 else pltpu.roll(ext, shift, 0)
            term = xs[CONV_HALO:CONV_HALO + n] * convw_ref[j:j + 1, lanes]
            acc = term if acc is None else acc + term
        cols.append(acc * _sigmoid(acc))
    y = jnp.concatenate(cols, axis=1)

    ba = ba_ref[0]
    beta_all = _sigmoid(ba)
    sp_arg = ba + dtb_ref[...]
    softplus = jnp.maximum(sp_arg, 0.0) + jnp.log(1.0 + jnp.exp(-jnp.abs(sp_arg)))
    g2 = (-LOG2E) * jnp.exp(alog_ref[...]) * softplus

    in_chunk = lax.broadcasted_iota(jnp.int32, (n, LANES), 0) % DN_CHUNK
    g_fwd = g2
    shift = 1
    while shift < DN_CHUNK:
        g_fwd = g_fwd + jnp.where(in_chunk >= shift, pltpu.roll(g_fwd, shift, 0), 0.0)
        shift *= 2
    total = jnp.concatenate(
        [jnp.broadcast_to(g_fwd[(c + 1) * DN_CHUNK - 1:(c + 1) * DN_CHUNK, :], (DN_CHUNK, LANES))
         for c in range(nch)], axis=0)
    g_bwd = total - g_fwd + g2
    g_cols = (g_fwd, g_bwd)
    g_rows = (jnp.transpose(g_fwd), jnp.transpose(g_bwd))

    m = DN_SUB
    row = lax.broadcasted_iota(jnp.int32, (m, m), 0)
    col = lax.broadcasted_iota(jnp.int32, (m, m), 1)
    eye = jnp.where(row == col, 1.0, 0.0)
    subs = [slice(s * m, (s + 1) * m) for s in range(n // m)]

    qs, ks, vs, kk, qk_raw = [], [], [], [], []
    for h in range(DN_HEADS):
        q = y[:, h * DN_HEAD:(h + 1) * DN_HEAD]
        k = y[:, (DN_HEADS + h) * DN_HEAD:(DN_HEADS + h + 1) * DN_HEAD]
        q = q * lax.rsqrt(jnp.sum(q * q, axis=-1, keepdims=True) + EPS) * (DN_HEAD ** -0.5)
        k = k * lax.rsqrt(jnp.sum(k * k, axis=-1, keepdims=True) + EPS)
        k_bf = k.astype(BF16)
        q_bf = q.astype(BF16)
        qs.append(q)
        ks.append(k)
        vs.append(y[:, (2 * DN_HEADS + h) * DN_HEAD:(2 * DN_HEADS + h + 1) * DN_HEAD])
        kk.append([_dot_nt(k_bf[r], k_bf[r]) for r in subs])
        qk_raw.append([_dot_nt(q_bf[r], k_bf[r]) for r in subs])

    hds = [(d, h) for d in range(DN_DIRS) for h in range(DN_HEADS)]
    prob = [(i, s) for i in range(len(hds)) for s in range(len(subs))]
    a_bf, x_inv, qk_bf, rhs_bf, qd, g_tot = {}, {}, {}, [], [], []
    for i, (d, h) in enumerate(hds):
        lane = 2 * DN_HEADS + i
        gc = jnp.broadcast_to(g_cols[d][:, lane:lane + 1], (n, LANES))
        beta = jnp.broadcast_to(beta_all[:, i:i + 1], (n, LANES))
        for s, r in enumerate(subs):
            g_row = g_rows[d][lane:lane + 1, r]
            decay = jnp.exp2(jnp.minimum(gc[r] - g_row, 0.0))
            a_mat = kk[h][s] * beta[r] * (decay * mask_ref[d * N_MASKS + 1])
            a_bf[i, s] = a_mat.astype(BF16)
            x_inv[i, s] = eye - a_mat * mask_ref[d * N_MASKS + 2]
            qk_bf[i, s] = (qk_raw[h][s] * (decay * mask_ref[d * N_MASKS])).astype(BF16)
        e_gc = jnp.exp2(gc)
        rhs_bf.append(jnp.concatenate([vs[h] * beta, ks[h] * beta * e_gc], axis=1).astype(BF16))
        qd.append(qs[h] * e_gc)

        edge = 0 if d else DN_CHUNK - 1
        gl_rows = [gc[c * DN_CHUNK + edge:c * DN_CHUNK + edge + 1, :] for c in range(nch)]
        gl = jnp.concatenate([jnp.broadcast_to(t, (DN_CHUNK, LANES)) for t in gl_rows], axis=0)
        g_tot.append([jnp.exp2(t) for t in gl_rows])
        kdt_ref[0, i] = jnp.transpose(ks[h] * jnp.exp2(gl - gc)).astype(BF16)

    for lb in range(1, DN_LEVELS):
        for i, s in prob:
            x_bf = x_inv[i, s].astype(BF16)
            upd = _dot(_dot(x_bf, a_bf[i, s]).astype(BF16), x_bf)
            x_inv[i, s] = x_inv[i, s] - upd * mask_ref[hds[i][0] * N_MASKS + 2 + lb]

    sols = {(i, s): _dot(x_inv[i, s].astype(BF16), rhs_bf[i][subs[s]]) for i, s in prob}
    corrs = {(i, s): _dot(qk_bf[i, s], sols[i, s].astype(BF16)) for i, s in prob}
    for i in range(len(hds)):
        sol = jnp.concatenate([sols[i, s] for s in range(len(subs))], axis=0)
        corr = jnp.concatenate([corrs[i, s] for s in range(len(subs))], axis=0)
        w_bf = sol[:, DN_HEAD:].astype(BF16)
        qe_bf = (qd[i] - corr[:, DN_HEAD:]).astype(BF16)
        pieces = []
        for c in range(nch):
            sl = slice(c * DN_CHUNK, (c + 1) * DN_CHUNK)
            pieces += [w_bf[sl], qe_bf[sl]]
        wq_ref[0, i] = jnp.concatenate(pieces, axis=0)
        u_ref[0, i] = sol[:, :DN_HEAD].astype(BF16)
        ot_ref[0, i] = corr[:, :DN_HEAD].astype(BF16)

    for c in range(nch):
        gt_ref[0, c] = jnp.concatenate([t[c] for t in g_tot], axis=0)


def _dn_local(dqkv, ba, conv_w, alog_pad, dtb_pad, masks):
    b, s, _ = dqkv.shape
    n = DN_BLOCK
    nblk = s // n
    per = n // CONV_HALO
    nhalo = s // CONV_HALO
    hds = DN_DIRS * DN_HEADS
    return pl.pallas_call(
        _dn_local_kernel,
        grid=(b, nblk),
        in_specs=[
            pl.BlockSpec((1, n, DN_QKV_W), lambda bi, c: (bi, c, 0)),
            pl.BlockSpec((1, CONV_HALO, DN_QKV_W), lambda bi, c: (bi, jnp.maximum(c * per - 1, 0), 0)),
            pl.BlockSpec((1, CONV_HALO, DN_QKV_W),
                         lambda bi, c: (bi, jnp.minimum((c + 1) * per, nhalo - 1), 0)),
            pl.BlockSpec((1, n, LANES), lambda bi, c: (bi, c, 0)),
            pl.BlockSpec((CONV_W, DN_QKV_W), lambda bi, c: (0, 0)),
            pl.BlockSpec((1, LANES), lambda bi, c: (0, 0)),
            pl.BlockSpec((1, LANES), lambda bi, c: (0, 0)),
            pl.BlockSpec(masks.shape, lambda bi, c: (0, 0, 0)),
        ],
        out_specs=[
            pl.BlockSpec((1, hds, 2 * n, DN_HEAD), lambda bi, c: (bi, 0, c, 0)),
            pl.BlockSpec((1, hds, n, DN_HEAD), lambda bi, c: (bi, 0, c, 0)),
            pl.BlockSpec((1, hds, n, DN_HEAD), lambda bi, c: (bi, 0, c, 0)),
            pl.BlockSpec((1, hds, DN_HEAD, n), lambda bi, c: (bi, 0, 0, c)),
            pl.BlockSpec((1, n // DN_CHUNK, hds, LANES), lambda bi, c: (bi, c, 0, 0)),
        ],
        out_shape=[
            jax.ShapeDtypeStruct((b, hds, 2 * s, DN_HEAD), BF16),
            jax.ShapeDtypeStruct((b, hds, s, DN_HEAD), BF16),
            jax.ShapeDtypeStruct((b, hds, s, DN_HEAD), BF16),
            jax.ShapeDtypeStruct((b, hds, DN_HEAD, s), BF16),
            jax.ShapeDtypeStruct((b, s // DN_CHUNK, hds, LANES), F32),
        ],
        compiler_params=pltpu.CompilerParams(
            dimension_semantics=("parallel", "parallel"), vmem_limit_bytes=VMEM_LIMIT),
        name="deltanet_local",
    )(dqkv, dqkv, dqkv, ba, conv_w, alog_pad, dtb_pad, masks)


def _dn_scan_kernel(wqf_ref, wqb_ref, uf_ref, ub_ref, otf_ref, otb_ref, kdf_ref, kdb_ref,
                    gtf_ref, gtb_ref, of_ref, ob_ref, s_ref):
    n = DN_BLOCK
    nch = n // DN_CHUNK

    @pl.when(pl.program_id(1) == 0)
    def _():
        s_ref[...] = jnp.zeros(s_ref.shape, F32)

    refs = ((wqf_ref, uf_ref, otf_ref, kdf_ref, gtf_ref), (wqb_ref, ub_ref, otb_ref, kdb_ref, gtb_ref))
    hds = DN_DIRS * DN_HEADS
    states = [s_ref[hd] for hd in range(hds)]
    outs = [[None] * nch for _ in range(hds)]
    for r in range(nch):
        for d in range(DN_DIRS):
            wq_ref, u_ref, ot_ref, kd_ref, gt_ref = refs[d]
            c = nch - 1 - r if d else r
            rows = slice(c * DN_CHUNK, (c + 1) * DN_CHUNK)
            for h in range(DN_HEADS):
                hd = d * DN_HEADS + h
                s_bf = states[hd].astype(BF16)
                res = _dot(wq_ref[0, h, 2 * c * DN_CHUNK:2 * (c + 1) * DN_CHUNK, :], s_bf)
                vn = u_ref[0, h, rows, :].astype(F32) - res[:DN_CHUNK]
                outs[hd][c] = res[DN_CHUNK:] + ot_ref[0, h, rows, :].astype(F32)
                pieces = []
                if c:
                    pieces.append(jnp.zeros((c * DN_CHUNK, LANES), BF16))
                pieces.append(vn.astype(BF16))
                if nch - 1 - c:
                    pieces.append(jnp.zeros(((nch - 1 - c) * DN_CHUNK, LANES), BF16))
                vn_pad = jnp.concatenate(pieces, axis=0)
                states[hd] = states[hd] * gt_ref[0, c, hd:hd + 1, :] + _dot(kd_ref[0, h], vn_pad)
    for hd in range(hds):
        s_ref[hd] = states[hd]
    for d, o_ref in enumerate((of_ref, ob_ref)):
        for h in range(DN_HEADS):
            o_ref[0, :, h * DN_HEAD:(h + 1) * DN_HEAD] = jnp.concatenate(
                outs[d * DN_HEADS + h], axis=0).astype(BF16)


def _dn_scan(wq, u, ot, kdt, gt):
    b, hds, s, _ = u.shape
    n = DN_BLOCK
    nblk = s // n
    nch = n // DN_CHUNK
    fwd = lambda bi, c: c
    bwd = lambda bi, c: nblk - 1 - c

    def pair(shape, index):
        return [pl.BlockSpec(shape, functools.partial(index, 0, fwd)),
                pl.BlockSpec(shape, functools.partial(index, 1, bwd))]

    rows = lambda d, pos, bi, c: (bi, d, pos(bi, c), 0)
    lanes = lambda d, pos, bi, c: (bi, d, 0, pos(bi, c))
    gts = lambda d, pos, bi, c: (bi, pos(bi, c), 0, 0)
    return pl.pallas_call(
        _dn_scan_kernel,
        grid=(b, nblk),
        in_specs=(pair((1, DN_HEADS, 2 * n, DN_HEAD), rows) + pair((1, DN_HEADS, n, DN_HEAD), rows)
                  + pair((1, DN_HEADS, n, DN_HEAD), rows) + pair((1, DN_HEADS, DN_HEAD, n), lanes)
                  + pair((1, nch, hds, LANES), gts)),
        out_specs=[
            pl.BlockSpec((1, n, DN_V_W), lambda bi, c: (bi, c, 0)),
            pl.BlockSpec((1, n, DN_V_W), lambda bi, c: (bi, nblk - 1 - c, 0)),
        ],
        out_shape=[jax.ShapeDtypeStruct((b, s, DN_V_W), BF16)] * 2,
        scratch_shapes=[pltpu.VMEM((hds, DN_HEAD, DN_HEAD), F32)],
        compiler_params=pltpu.CompilerParams(
            dimension_semantics=("parallel", "arbitrary"), vmem_limit_bytes=VMEM_LIMIT),
        name="deltanet_scan",
    )(wq, wq, u, u, ot, ot, kdt, kdt, gt, gt)


def _post_kernel(ff_chunk, final_norm, x_ref, attn_ref, of_ref, ob_ref, dz_ref, p_ref, dng_ref, wout_ref,
                 nffn_ref, wg_ref, wu_ref, wd_ref, nple_ref, wpg_ref, wple_ref, nfin_ref, out_ref):
    o = of_ref[...].astype(F32) + ob_ref[...].astype(F32)
    z = dz_ref[...].astype(F32)
    parts = [attn_ref[...]]
    for h in range(DN_HEADS):
        sl = slice(h * DN_HEAD, (h + 1) * DN_HEAD)
        zh = z[:, sl]
        parts.append((_rms(o[:, sl], dng_ref[...]) * (zh * _sigmoid(zh))).astype(BF16))
    mixed = jnp.concatenate(parts, axis=1)
    h1 = x_ref[...] + _dot(mixed, wout_ref[...])

    hn = _rms(h1, nffn_ref[...]).astype(BF16)
    h2 = h1
    ff = wg_ref.shape[1]
    for c in range(ff // ff_chunk):
        sl = slice(c * ff_chunk, (c + 1) * ff_chunk)
        gate = _dot(hn, wg_ref[:, sl])
        up = _dot(hn, wu_ref[:, sl])
        act = (gate * _sigmoid(gate) * up).astype(BF16)
        h2 = h2 + _dot(act, wd_ref[sl, :])

    hp = _rms(h2, nple_ref[...]).astype(BF16)
    gate = _sigmoid(_dot(hp, wpg_ref[...]))
    h3 = h2 + gate * _dot(p_ref[...].astype(BF16), wple_ref[...])
    out_ref[...] = _rms(h3, nfin_ref[...]) if final_norm else h3


def _post(xf, attn, o_f, o_b, dz, pf, dng, w_out, nffn, w_gate, w_up, w_down, nple, w_pg, w_ple,
          nfin, tm, ff_chunk, final_norm):
    t, d = xf.shape
    row = lambda i: (i, 0)
    const = lambda i: (0, 0)

    def resident(shape):
        return pl.BlockSpec(shape, const, pipeline_mode=pl.Buffered(1))

    ff = w_gate.shape[1]
    return pl.pallas_call(
        functools.partial(_post_kernel, ff_chunk, final_norm),
        grid=(t // tm,),
        in_specs=[
            pl.BlockSpec((tm, d), row),
            pl.BlockSpec((tm, ATTN_Q_W), row),
            pl.BlockSpec((tm, DN_V_W), row),
            pl.BlockSpec((tm, DN_V_W), row),
            pl.BlockSpec((tm, DN_V_W), row),
            pl.BlockSpec((tm, pf.shape[1]), row),
            resident((1, DN_HEAD)),
            resident((ATTN_Q_W + DN_V_W, d)),
            resident((1, d)),
            resident((d, ff)),
            resident((d, ff)),
            resident((ff, d)),
            resident((1, d)),
            resident((d, d)),
            resident((pf.shape[1], d)),
            resident((1, d)),
        ],
        out_specs=pl.BlockSpec((tm, d), row),
        out_shape=jax.ShapeDtypeStruct((t, d), F32),
        compiler_params=pltpu.CompilerParams(
            dimension_semantics=("parallel",), vmem_limit_bytes=VMEM_LIMIT),
        name="post",
    )(xf, attn, o_f, o_b, dz, pf, dng, w_out, nffn, w_gate, w_up, w_down, nple, w_pg, w_ple, nfin)


def _rope_tables(seq):
    pos = jnp.arange(seq, dtype=jnp.int32)
    rowcol = jnp.stack([(pos // GRID_W).astype(F32), (pos % GRID_W).astype(F32)], axis=1)
    sec = ATTN_HEAD_DIM // 2
    inv_freq = ROPE_THETA ** (-jnp.arange(0, sec, 2, dtype=F32) / sec)
    d = np.arange(LANES) % ATTN_HEAD_DIM
    axis = d // sec
    idx = d % (sec // 2)
    sign = np.where((d % sec) < sec // 2, -1.0, 1.0).astype(np.float32)
    ang = rowcol[:, axis] * inv_freq[idx][None, :]
    return jnp.cos(ang), jnp.sin(ang) * sign[None, :]


def kernel(x, p, norm_mix, w_in, conv_w, q_norm, k_norm, a_log, dt_bias, dn_norm, w_out,
           norm_ffn, w_gate, w_up, w_down, norm_ple, w_ple_gate, w_ple, norm_final):
    b, s, d = x.shape
    depth = w_in.shape[0]
    t = b * s
    tm, tq = min(ROW_TILE, s), min(Q_TILE, s)
    cos_t, sin_t = _rope_tables(s)

    h = x.reshape(t, d)
    for i in range(depth):
        wi = w_in[i].astype(BF16)
        w_cat = jnp.pad(wi, ((0, 0), (0, LANES - 4 * DN_HEADS)))

        qg = jnp.tile(q_norm[i], LANES // ATTN_HEAD_DIM)[None, :]
        kg = jnp.tile(k_norm[i], LANES // ATTN_HEAD_DIM)[None, :]
        q, k, va, vb, dqkv, dz, ba = _inproj(h, norm_mix[i][None, :], w_cat, cos_t, sin_t, qg, kg, s, tm)

        score_bound = (ATTN_HEAD_DIM ** 0.5) * jnp.max(jnp.abs(q_norm[i])) * jnp.max(jnp.abs(k_norm[i]))
        bounded = (score_bound <= MAX_UNSHIFTED_SCORE).astype(jnp.int32).reshape(1)
        attn = _attention(bounded, q.reshape(b, s, ATTN_Q_W), k.reshape(b, s // tm, tm, KV_W),
                          va.reshape(b, s // tm, V_ROWS, tm), vb.reshape(b, s // tm, V_ROWS, tm), tq)

        pad = (2 * DN_HEADS, LANES - 4 * DN_HEADS)
        alog_pad = jnp.pad(a_log[i], pad)[None, :]
        dtb_pad = jnp.pad(dt_bias[i], pad)[None, :]
        dqkv3 = dqkv.reshape(b, s, DN_QKV_W)
        ba3 = ba.reshape(b, s, LANES)
        o_f, o_b = _dn_scan(*_dn_local(dqkv3, ba3, conv_w[i], alog_pad, dtb_pad, jnp.asarray(_dn_masks())))

        h = _post(h, attn.reshape(t, ATTN_Q_W), o_f.reshape(t, DN_V_W), o_b.reshape(t, DN_V_W), dz,
                  p[i].reshape(t, -1), dn_norm[i][None, :], w_out[i].astype(BF16),
                  norm_ffn[i][None, :], w_gate[i].astype(BF16), w_up[i].astype(BF16),
                  w_down[i].astype(BF16), norm_ple[i][None, :], w_ple_gate[i].astype(BF16),
                  w_ple[i].astype(BF16),
                  norm_final[None, :], tm, FF_CHUNK, i == depth - 1)
    return h.reshape(b, s, d)
```

```python
import functools

import numpy as np
import jax
import jax.numpy as jnp
from jax import lax
from jax.experimental import pallas as pl
from jax.experimental.pallas import tpu as pltpu

F32 = jnp.float32
BF16 = jnp.bfloat16

LANES = 128
SUBLANES = 8
VMEM_LIMIT = 60000 * 1024

GRID_W = 64
ATTN_HEADS = 8
ATTN_KV_HEADS = 2
ATTN_HEAD_DIM = 64
ATTN_GROUP = ATTN_HEADS // ATTN_KV_HEADS
ROPE_THETA = 10000.0
DN_HEADS = 4
DN_HEAD = 128
DN_CHUNK = 128
DN_DIRS = 2
DN_LEVELS = 7
N_MASKS = 2 + DN_LEVELS
CONV_W = 5
EPS = 1e-6
LOG2E = float(np.log2(np.e))

KV_W = ATTN_KV_HEADS * ATTN_HEAD_DIM
DN_QKV_W = 3 * DN_HEADS * DN_HEAD
DN_V_W = DN_HEADS * DN_HEAD
ATTN_Q_W = ATTN_HEADS * ATTN_HEAD_DIM

ROW_TILE = 512
Q_TILE = 512
KV_UNROLL = 8
V_ROWS = ATTN_HEAD_DIM + 16
Q_SCALE = ATTN_HEAD_DIM ** -0.5 * LOG2E
MAX_UNSHIFTED_SCORE = 40.0
FF_CHUNK = 256
DN_BLOCK = 256
DN_SUB = LANES
CONV_HALO = SUBLANES


def _dot(a, b):
    return jnp.dot(a, b, preferred_element_type=F32)


def _dot_nt(a, b):
    return lax.dot_general(a, b, (((1,), (1,)), ((), ())), preferred_element_type=F32)


def _rms(x, gain):
    return x * lax.rsqrt(jnp.mean(x * x, axis=-1, keepdims=True) + EPS) * gain


def _sigmoid(x):
    return 1.0 / (1.0 + jnp.exp(-x))


def _inproj_kernel(x_ref, gain_ref, w_ref, cos_ref, sin_ref, qg_ref, kg_ref,
                   q_ref, k_ref, va_ref, vb_ref, dqkv_ref, dz_ref, ba_ref):
    tm = x_ref.shape[0]
    hb = _rms(x_ref[...], gain_ref[...]).astype(BF16)
    cos = cos_ref[...]
    sin = sin_ref[...]
    lane = lax.broadcasted_iota(jnp.int32, (tm, LANES), 1)
    low16 = (lane % 32) < 16
    head0 = lane < ATTN_HEAD_DIM

    def norm_rope(t, gain):
        t2 = t * t
        ss0 = jnp.sum(jnp.where(head0, t2, 0.0), axis=-1, keepdims=True) * (1.0 / ATTN_HEAD_DIM)
        ss1 = jnp.sum(jnp.where(head0, 0.0, t2), axis=-1, keepdims=True) * (1.0 / ATTN_HEAD_DIM)
        tn = t * jnp.where(head0, lax.rsqrt(ss0 + EPS), lax.rsqrt(ss1 + EPS)) * gain
        partner = jnp.where(low16, pltpu.roll(tn, LANES - 16, 1), pltpu.roll(tn, 16, 1))
        return tn * cos + partner * sin

    c0 = 0
    q_all = _dot(hb, w_ref[:, c0:c0 + ATTN_Q_W])
    for c in range(ATTN_Q_W // LANES):
        sl = slice(c * LANES, (c + 1) * LANES)
        q_ref[:, sl] = (norm_rope(q_all[:, sl], qg_ref[...]) * Q_SCALE).astype(BF16)
    c0 += ATTN_Q_W

    kv = _dot(hb, w_ref[:, c0:c0 + 2 * KV_W])
    k_ref[...] = norm_rope(kv[:, :KV_W], kg_ref[...]).astype(BF16)
    v_t = jnp.transpose(kv[:, KV_W:])
    ones = jnp.ones((V_ROWS - ATTN_HEAD_DIM, tm), F32)
    va_ref[0] = jnp.concatenate([v_t[:ATTN_HEAD_DIM], ones], axis=0).astype(BF16)
    vb_ref[0] = jnp.concatenate([v_t[ATTN_HEAD_DIM:], ones], axis=0).astype(BF16)
    c0 += 2 * KV_W

    dqkv_ref[...] = _dot(hb, w_ref[:, c0:c0 + DN_QKV_W]).astype(BF16)
    c0 += DN_QKV_W
    zb = _dot(hb, w_ref[:, c0:c0 + DN_V_W + LANES])
    dz_ref[...] = zb[:, :DN_V_W].astype(BF16)
    ba_ref[...] = zb[:, DN_V_W:]


def _inproj(xf, gain, w_cat, cos_t, sin_t, qg, kg, seq, tm):
    t, d = xf.shape
    nseq = seq // tm
    wcols = w_cat.shape[1]
    row = lambda i: (i, 0)
    const = lambda i: (0, 0)
    tab = lambda i: (i % nseq, 0)
    return pl.pallas_call(
        _inproj_kernel,
        grid=(t // tm,),
        in_specs=[
            pl.BlockSpec((tm, d), row),
            pl.BlockSpec((1, d), const),
            pl.BlockSpec((d, wcols), const),
            pl.BlockSpec((tm, LANES), tab),
            pl.BlockSpec((tm, LANES), tab),
            pl.BlockSpec((1, LANES), const),
            pl.BlockSpec((1, LANES), const),
        ],
        out_specs=[
            pl.BlockSpec((tm, ATTN_Q_W), row),
            pl.BlockSpec((tm, KV_W), row),
            pl.BlockSpec((1, V_ROWS, tm), lambda i: (i, 0, 0)),
            pl.BlockSpec((1, V_ROWS, tm), lambda i: (i, 0, 0)),
            pl.BlockSpec((tm, DN_QKV_W), row),
            pl.BlockSpec((tm, DN_V_W), row),
            pl.BlockSpec((tm, LANES), row),
        ],
        out_shape=[
            jax.ShapeDtypeStruct((t, ATTN_Q_W), BF16),
            jax.ShapeDtypeStruct((t, KV_W), BF16),
            jax.ShapeDtypeStruct((t // tm, V_ROWS, tm), BF16),
            jax.ShapeDtypeStruct((t // tm, V_ROWS, tm), BF16),
            jax.ShapeDtypeStruct((t, DN_QKV_W), BF16),
            jax.ShapeDtypeStruct((t, DN_V_W), BF16),
            jax.ShapeDtypeStruct((t, LANES), F32),
        ],
        compiler_params=pltpu.CompilerParams(
            dimension_semantics=("parallel",), vmem_limit_bytes=VMEM_LIMIT),
        name="inproj",
    )(xf, gain, w_cat, cos_t, sin_t, qg, kg)


def _attn_kernel(bounded_ref, q_ref, k_ref, vat_ref, vbt_ref, o_ref, qt_ref, acc_ref, m_ref):
    tq = q_ref.shape[1]
    nkv = k_ref.shape[1]
    unroll = KV_UNROLL if nkv % KV_UNROLL == 0 else 1
    zeros = jnp.zeros((ATTN_HEAD_DIM, tq), BF16)
    for c in range(ATTN_Q_W // LANES):
        pair_t = jnp.transpose(q_ref[0, :, c * LANES:(c + 1) * LANES].astype(F32)).astype(BF16)
        for e in range(2):
            h = 2 * c + e
            g, j = divmod(h, ATTN_GROUP)
            q_t = pair_t[e * ATTN_HEAD_DIM:(e + 1) * ATTN_HEAD_DIM]
            qt_ref[g, :, j * tq:(j + 1) * tq] = jnp.concatenate(
                [q_t, zeros] if g == 0 else [zeros, q_t], axis=0)
    acc_ref[...] = jnp.zeros(acc_ref.shape, F32)
    vt_refs = (vat_ref, vbt_ref)

    @pl.when(bounded_ref[0] == 1)
    def _():
        def body(i, carry):
            for g in range(ATTN_KV_HEADS):
                upd = None
                for t in range(unroll):
                    tile = i * unroll + t
                    p_t = jnp.exp2(_dot(k_ref[0, tile], qt_ref[g])).astype(BF16)
                    part = _dot(vt_refs[g][0, tile], p_t)
                    upd = part if upd is None else upd + part
                acc_ref[g] += upd
            return carry

        lax.fori_loop(0, nkv // unroll, body, 0)

    @pl.when(bounded_ref[0] == 0)
    def _():
        m_ref[...] = jnp.full(m_ref.shape, -jnp.inf, F32)

        def body(i, carry):
            kt = k_ref[0, i]
            for g in range(ATTN_KV_HEADS):
                s_t = _dot(kt, qt_ref[g])
                m_prev = m_ref[g]
                m_new = jnp.maximum(m_prev, jnp.max(s_t, axis=0, keepdims=True))
                p_t = jnp.exp2(s_t - m_new[0:1, :]).astype(BF16)
                alpha = jnp.exp2(m_prev - m_new)[0:1, :]
                acc_ref[g] = alpha * acc_ref[g] + _dot(vt_refs[g][0, i], p_t)
                m_ref[g] = m_new
            return carry

        lax.fori_loop(0, nkv, body, 0)

    for g in range(ATTN_KV_HEADS):
        a = acc_ref[g]
        o_t = a[:ATTN_HEAD_DIM] / a[ATTN_HEAD_DIM:ATTN_HEAD_DIM + 1]
        for jp in range(ATTN_GROUP // 2):
            pair = jnp.concatenate(
                [o_t[:, (2 * jp) * tq:(2 * jp + 1) * tq], o_t[:, (2 * jp + 1) * tq:(2 * jp + 2) * tq]],
                axis=0)
            c = g * (ATTN_GROUP // 2) + jp
            o_ref[0, :, c * LANES:(c + 1) * LANES] = jnp.transpose(pair).astype(BF16)


def _attention(bounded, q, k, vat, vbt, tq):
    b, s, _ = q.shape
    _, nkv, tk, _ = k.shape
    cols = ATTN_GROUP * tq
    return pl.pallas_call(
        _attn_kernel,
        grid_spec=pltpu.PrefetchScalarGridSpec(
            num_scalar_prefetch=1,
            grid=(b, s // tq),
            in_specs=[
                pl.BlockSpec((1, tq, ATTN_Q_W), lambda bi, qi, flag: (bi, qi, 0)),
                pl.BlockSpec((1, nkv, tk, KV_W), lambda bi, qi, flag: (bi, 0, 0, 0)),
                pl.BlockSpec((1, nkv, V_ROWS, tk), lambda bi, qi, flag: (bi, 0, 0, 0)),
                pl.BlockSpec((1, nkv, V_ROWS, tk), lambda bi, qi, flag: (bi, 0, 0, 0)),
            ],
            out_specs=pl.BlockSpec((1, tq, ATTN_Q_W), lambda bi, qi, flag: (bi, qi, 0)),
            scratch_shapes=[
                pltpu.VMEM((ATTN_KV_HEADS, LANES, cols), BF16),
                pltpu.VMEM((ATTN_KV_HEADS, V_ROWS, cols), F32),
                pltpu.VMEM((ATTN_KV_HEADS, SUBLANES, cols), F32),
            ],
        ),
        out_shape=jax.ShapeDtypeStruct((b, s, ATTN_Q_W), BF16),
        compiler_params=pltpu.CompilerParams(
            dimension_semantics=("parallel", "arbitrary"), vmem_limit_bytes=VMEM_LIMIT),
        name="attention",
    )(bounded, q, k, vat, vbt)


def _dn_masks():
    n = DN_SUB
    row = np.arange(n)[:, None]
    col = np.arange(n)[None, :]
    same = (row // DN_CHUNK) == (col // DN_CHUNK)
    out = []
    for rev in (False, True):
        a, b = (col, row) if rev else (row, col)
        out.append(same & (a >= b))
        out.append(same & (a > b))
        for lb in range(DN_LEVELS):
            out.append(((a >> lb) == (b >> lb) + 1) & (((a >> lb) & 1) == 1))
    return np.stack(out).astype(np.float32)


def _dn_local_kernel(qkv_ref, prev_ref, next_ref, ba_ref, convw_ref, alog_ref, dtb_ref, mask_ref,
                     wq_ref, u_ref, ot_ref, kdt_ref, gt_ref):
    n = DN_BLOCK
    nch = n // DN_CHUNK
    blk = pl.program_id(1)
    nblk = pl.num_programs(1)

    rows = n + 2 * CONV_HALO
    cols = []
    for c in range(DN_QKV_W // LANES):
        lanes = slice(c * LANES, (c + 1) * LANES)
        top = jnp.where(blk == 0, 0.0, prev_ref[0, :, lanes].astype(F32))
        bot = jnp.where(blk == nblk - 1, 0.0, next_ref[0, :, lanes].astype(F32))
        ext = jnp.concatenate([top, qkv_ref[0, :, lanes].astype(F32), bot], axis=0)
        acc = None
        for j in range(CONV_W):
            shift = (CONV_W // 2 - j) % rows
            xs = ext if shift == 0 else pltpu.roll(ext, shift, 0)
            term = xs[CONV_HALO:CONV_HALO + n] * convw_ref[j:j + 1, lanes]
            acc = term if acc is None else acc + term
        cols.append(acc * _sigmoid(acc))
    y = jnp.concatenate(cols, axis=1)

    ba = ba_ref[0]
    beta_all = _sigmoid(ba)
    sp_arg = ba + dtb_ref[...]
    softplus = jnp.maximum(sp_arg, 0.0) + jnp.log(1.0 + jnp.exp(-jnp.abs(sp_arg)))
    g2 = (-LOG2E) * jnp.exp(alog_ref[...]) * softplus

    in_chunk = lax.broadcasted_iota(jnp.int32, (n, LANES), 0) % DN_CHUNK
    g_fwd = g2
    shift = 1
    while shift < DN_CHUNK:
        g_fwd = g_fwd + jnp.where(in_chunk >= shift, pltpu.roll(g_fwd, shift, 0), 0.0)
        shift *= 2
    total = jnp.concatenate(
        [jnp.broadcast_to(g_fwd[(c + 1) * DN_CHUNK - 1:(c + 1) * DN_CHUNK, :], (DN_CHUNK, LANES))
         for c in range(nch)], axis=0)
    g_bwd = total - g_fwd + g2
    g_cols = (g_fwd, g_bwd)
    g_rows = (jnp.transpose(g_fwd), jnp.transpose(g_bwd))

    m = DN_SUB
    row = lax.broadcasted_iota(jnp.int32, (m, m), 0)
    col = lax.broadcasted_iota(jnp.int32, (m, m), 1)
    eye = jnp.where(row == col, 1.0, 0.0)
    subs = [slice(s * m, (s + 1) * m) for s in range(n // m)]

    qs, ks, vs, kk, qk_raw = [], [], [], [], []
    for h in range(DN_HEADS):
        q = y[:, h * DN_HEAD:(h + 1) * DN_HEAD]
        k = y[:, (DN_HEADS + h) * DN_HEAD:(DN_HEADS + h + 1) * DN_HEAD]
        q = q * lax.rsqrt(jnp.sum(q * q, axis=-1, keepdims=True) + EPS) * (DN_HEAD ** -0.5)
        k = k * lax.rsqrt(jnp.sum(k * k, axis=-1, keepdims=True) + EPS)
        k_bf = k.astype(BF16)
        q_bf = q.astype(BF16)
        qs.append(q)
        ks.append(k)
        vs.append(y[:, (2 * DN_HEADS + h) * DN_HEAD:(2 * DN_HEADS + h + 1) * DN_HEAD])
        kk.append([_dot_nt(k_bf[r], k_bf[r]) for r in subs])
        qk_raw.append([_dot_nt(q_bf[r], k_bf[r]) for r in subs])

    hds = [(d, h) for d in range(DN_DIRS) for h in range(DN_HEADS)]
    prob = [(i, s) for i in range(len(hds)) for s in range(len(subs))]
    a_bf, x_inv, qk_bf, rhs_bf, qd, g_tot = {}, {}, {}, [], [], []
    for i, (d, h) in enumerate(hds):
        lane = 2 * DN_HEADS + i
        gc = jnp.broadcast_to(g_cols[d][:, lane:lane + 1], (n, LANES))
        beta = jnp.broadcast_to(beta_all[:, i:i + 1], (n, LANES))
        for s, r in enumerate(subs):
            g_row = g_rows[d][lane:lane + 1, r]
            decay = jnp.exp2(jnp.minimum(gc[r] - g_row, 0.0))
            a_mat = kk[h][s] * beta[r] * (decay * mask_ref[d * N_MASKS + 1])
            a_bf[i, s] = a_mat.astype(BF16)
            x_inv[i, s] = eye - a_mat * mask_ref[d * N_MASKS + 2]
            qk_bf[i, s] = (qk_raw[h][s] * (decay * mask_ref[d * N_MASKS])).astype(BF16)
        e_gc = jnp.exp2(gc)
        rhs_bf.append(jnp.concatenate([vs[h] * beta, ks[h] * beta * e_gc], axis=1).astype(BF16))
        qd.append(qs[h] * e_gc)

        edge = 0 if d else DN_CHUNK - 1
        gl_rows = [gc[c * DN_CHUNK + edge:c * DN_CHUNK + edge + 1, :] for c in range(nch)]
        gl = jnp.concatenate([jnp.broadcast_to(t, (DN_CHUNK, LANES)) for t in gl_rows], axis=0)
        g_tot.append([jnp.exp2(t) for t in gl_rows])
        kdt_ref[0, i] = jnp.transpose(ks[h] * jnp.exp2(gl - gc)).astype(BF16)

    for lb in range(1, DN_LEVELS):
        for i, s in prob:
            x_bf = x_inv[i, s].astype(BF16)
            upd = _dot(_dot(x_bf, a_bf[i, s]).astype(BF16), x_bf)
            x_inv[i, s] = x_inv[i, s] - upd * mask_ref[hds[i][0] * N_MASKS + 2 + lb]

    sols = {(i, s): _dot(x_inv[i, s].astype(BF16), rhs_bf[i][subs[s]]) for i, s in prob}
    corrs = {(i, s): _dot(qk_bf[i, s], sols[i, s].astype(BF16)) for i, s in prob}
    for i in range(len(hds)):
        sol = jnp.concatenate([sols[i, s] for s in range(len(subs))], axis=0)
        corr = jnp.concatenate([corrs[i, s] for s in range(len(subs))], axis=0)
        w_bf = sol[:, DN_HEAD:].astype(BF16)
        qe_bf = (qd[i] - corr[:, DN_HEAD:]).astype(BF16)
        pieces = []
        for c in range(nch):
            sl = slice(c * DN_CHUNK, (c + 1) * DN_CHUNK)
            pieces += [w_bf[sl], qe_bf[sl]]
        wq_ref[0, i] = jnp.concatenate(pieces, axis=0)
        u_ref[0, i] = sol[:, :DN_HEAD].astype(BF16)
        ot_ref[0, i] = corr[:, :DN_HEAD].astype(BF16)

    for c in range(nch):
        gt_ref[0, c] = jnp.concatenate([t[c] for t in g_tot], axis=0)


def _dn_local(dqkv, ba, conv_w, alog_pad, dtb_pad, masks):
    b, s, _ = dqkv.shape
    n = DN_BLOCK
    nblk = s // n
    per = n // CONV_HALO
    nhalo = s // CONV_HALO
    hds = DN_DIRS * DN_HEADS
    return pl.pallas_call(
        _dn_local_kernel,
        grid=(b, nblk),
        in_specs=[
            pl.BlockSpec((1, n, DN_QKV_W), lambda bi, c: (bi, c, 0)),
            pl.BlockSpec((1, CONV_HALO, DN_QKV_W), lambda bi, c: (bi, jnp.maximum(c * per - 1, 0), 0)),
            pl.BlockSpec((1, CONV_HALO, DN_QKV_W),
                         lambda bi, c: (bi, jnp.minimum((c + 1) * per, nhalo - 1), 0)),
            pl.BlockSpec((1, n, LANES), lambda bi, c: (bi, c, 0)),
            pl.BlockSpec((CONV_W, DN_QKV_W), lambda bi, c: (0, 0)),
            pl.BlockSpec((1, LANES), lambda bi, c: (0, 0)),
            pl.BlockSpec((1, LANES), lambda bi, c: (0, 0)),
            pl.BlockSpec(masks.shape, lambda bi, c: (0, 0, 0)),
        ],
        out_specs=[
            pl.BlockSpec((1, hds, 2 * n, DN_HEAD), lambda bi, c: (bi, 0, c, 0)),
            pl.BlockSpec((1, hds, n, DN_HEAD), lambda bi, c: (bi, 0, c, 0)),
            pl.BlockSpec((1, hds, n, DN_HEAD), lambda bi, c: (bi, 0, c, 0)),
            pl.BlockSpec((1, hds, DN_HEAD, n), lambda bi, c: (bi, 0, 0, c)),
            pl.BlockSpec((1, n // DN_CHUNK, hds, LANES), lambda bi, c: (bi, c, 0, 0)),
        ],
        out_shape=[
            jax.ShapeDtypeStruct((b, hds, 2 * s, DN_HEAD), BF16),
            jax.ShapeDtypeStruct((b, hds, s, DN_HEAD), BF16),
            jax.ShapeDtypeStruct((b, hds, s, DN_HEAD), BF16),
            jax.ShapeDtypeStruct((b, hds, DN_HEAD, s), BF16),
            jax.ShapeDtypeStruct((b, s // DN_CHUNK, hds, LANES), F32),
        ],
        compiler_params=pltpu.CompilerParams(
            dimension_semantics=("parallel", "parallel"), vmem_limit_bytes=VMEM_LIMIT),
        name="deltanet_local",
    )(dqkv, dqkv, dqkv, ba, conv_w, alog_pad, dtb_pad, masks)


def _dn_scan_kernel(wqf_ref, wqb_ref, uf_ref, ub_ref, otf_ref, otb_ref, kdf_ref, kdb_ref,
                    gtf_ref, gtb_ref, of_ref, ob_ref, s_ref):
    n = DN_BLOCK
    nch = n // DN_CHUNK

    @pl.when(pl.program_id(1) == 0)
    def _():
        s_ref[...] = jnp.zeros(s_ref.shape, F32)

    refs = ((wqf_ref, uf_ref, otf_ref, kdf_ref, gtf_ref), (wqb_ref, ub_ref, otb_ref, kdb_ref, gtb_ref))
    hds = DN_DIRS * DN_HEADS
    states = [s_ref[hd] for hd in range(hds)]
    outs = [[None] * nch for _ in range(hds)]
    for r in range(nch):
        for d in range(DN_DIRS):
            wq_ref, u_ref, ot_ref, kd_ref, gt_ref = refs[d]
            c = nch - 1 - r if d else r
            rows = slice(c * DN_CHUNK, (c + 1) * DN_CHUNK)
            for h in range(DN_HEADS):
                hd = d * DN_HEADS + h
                s_bf = states[hd].astype(BF16)
                res = _dot(wq_ref[0, h, 2 * c * DN_CHUNK:2 * (c + 1) * DN_CHUNK, :], s_bf)
                vn = u_ref[0, h, rows, :].astype(F32) - res[:DN_CHUNK]
                outs[hd][c] = res[DN_CHUNK:] + ot_ref[0, h, rows, :].astype(F32)
                pieces = []
                if c:
                    pieces.append(jnp.zeros((c * DN_CHUNK, LANES), BF16))
                pieces.append(vn.astype(BF16))
                if nch - 1 - c:
                    pieces.append(jnp.zeros(((nch - 1 - c) * DN_CHUNK, LANES), BF16))
                vn_pad = jnp.concatenate(pieces, axis=0)
                states[hd] = states[hd] * gt_ref[0, c, hd:hd + 1, :] + _dot(kd_ref[0, h], vn_pad)
    for hd in range(hds):
        s_ref[hd] = states[hd]
    for d, o_ref in enumerate((of_ref, ob_ref)):
        for h in range(DN_HEADS):
            o_ref[0, :, h * DN_HEAD:(h + 1) * DN_HEAD] = jnp.concatenate(
                outs[d * DN_HEADS + h], axis=0).astype(BF16)


def _dn_scan(wq, u, ot, kdt, gt):
    b, hds, s, _ = u.shape
    n = DN_BLOCK
    nblk = s // n
    nch = n // DN_CHUNK
    fwd = lambda bi, c: c
    bwd = lambda bi, c: nblk - 1 - c

    def pair(shape, index):
        return [pl.BlockSpec(shape, functools.partial(index, 0, fwd)),
                pl.BlockSpec(shape, functools.partial(index, 1, bwd))]

    rows = lambda d, pos, bi, c: (bi, d, pos(bi, c), 0)
    lanes = lambda d, pos, bi, c: (bi, d, 0, pos(bi, c))
    gts = lambda d, pos, bi, c: (bi, pos(bi, c), 0, 0)
    return pl.pallas_call(
        _dn_scan_kernel,
        grid=(b, nblk),
        in_specs=(pair((1, DN_HEADS, 2 * n, DN_HEAD), rows) + pair((1, DN_HEADS, n, DN_HEAD), rows)
                  + pair((1, DN_HEADS, n, DN_HEAD), rows) + pair((1, DN_HEADS, DN_HEAD, n), lanes)
                  + pair((1, nch, hds, LANES), gts)),
        out_specs=[
            pl.BlockSpec((1, n, DN_V_W), lambda bi, c: (bi, c, 0)),
            pl.BlockSpec((1, n, DN_V_W), lambda bi, c: (bi, nblk - 1 - c, 0)),
        ],
        out_shape=[jax.ShapeDtypeStruct((b, s, DN_V_W), BF16)] * 2,
        scratch_shapes=[pltpu.VMEM((hds, DN_HEAD, DN_HEAD), F32)],
        compiler_params=pltpu.CompilerParams(
            dimension_semantics=("parallel", "arbitrary"), vmem_limit_bytes=VMEM_LIMIT),
        name="deltanet_scan",
    )(wq, wq, u, u, ot, ot, kdt, kdt, gt, gt)


def _post_kernel(ff_chunk, final_norm, x_ref, attn_ref, of_ref, ob_ref, dz_ref, p_ref, dng_ref, wout_ref,
                 nffn_ref, wg_ref, wu_ref, wd_ref, nple_ref, wpg_ref, wple_ref, nfin_ref, out_ref):
    o = of_ref[...].astype(F32) + ob_ref[...].astype(F32)
    z = dz_ref[...].astype(F32)
    parts = [attn_ref[...]]
    for h in range(DN_HEADS):
        sl = slice(h * DN_HEAD, (h + 1) * DN_HEAD)
        zh = z[:, sl]
        parts.append((_rms(o[:, sl], dng_ref[...]) * (zh * _sigmoid(zh))).astype(BF16))
    mixed = jnp.concatenate(parts, axis=1)
    h1 = x_ref[...] + _dot(mixed, wout_ref[...])

    hn = _rms(h1, nffn_ref[...]).astype(BF16)
    h2 = h1
    ff = wg_ref.shape[1]
    for c in range(ff // ff_chunk):
        sl = slice(c * ff_chunk, (c + 1) * ff_chunk)
        gate = _dot(hn, wg_ref[:, sl])
        up = _dot(hn, wu_ref[:, sl])
        act = (gate * _sigmoid(gate) * up).astype(BF16)
        h2 = h2 + _dot(act, wd_ref[sl, :])

    hp = _rms(h2, nple_ref[...]).astype(BF16)
    gate = _sigmoid(_dot(hp, wpg_ref[...]))
    h3 = h2 + gate * _dot(p_ref[...].astype(BF16), wple_ref[...])
    out_ref[...] = _rms(h3, nfin_ref[...]) if final_norm else h3


def _post(xf, attn, o_f, o_b, dz, pf, dng, w_out, nffn, w_gate, w_up, w_down, nple, w_pg, w_ple,
          nfin, tm, ff_chunk, final_norm):
    t, d = xf.shape
    row = lambda i: (i, 0)
    const = lambda i: (0, 0)

    def resident(shape):
        return pl.BlockSpec(shape, const, pipeline_mode=pl.Buffered(1))

    ff = w_gate.shape[1]
    return pl.pallas_call(
        functools.partial(_post_kernel, ff_chunk, final_norm),
        grid=(t // tm,),
        in_specs=[
            pl.BlockSpec((tm, d), row),
            pl.BlockSpec((tm, ATTN_Q_W), row),
            pl.BlockSpec((tm, DN_V_W), row),
            pl.BlockSpec((tm, DN_V_W), row),
            pl.BlockSpec((tm, DN_V_W), row),
            pl.BlockSpec((tm, pf.shape[1]), row),
            resident((1, DN_HEAD)),
            resident((ATTN_Q_W + DN_V_W, d)),
            resident((1, d)),
            resident((d, ff)),
            resident((d, ff)),
            resident((ff, d)),
            resident((1, d)),
            resident((d, d)),
            resident((pf.shape[1], d)),
            resident((1, d)),
        ],
        out_specs=pl.BlockSpec((tm, d), row),
        out_shape=jax.ShapeDtypeStruct((t, d), F32),
        compiler_params=pltpu.CompilerParams(
            dimension_semantics=("parallel",), vmem_limit_bytes=VMEM_LIMIT),
        name="post",
    )(xf, attn, o_f, o_b, dz, pf, dng, w_out, nffn, w_gate, w_up, w_down, nple, w_pg, w_ple, nfin)


def _rope_tables(seq):
    f32 = np.float32
    pos = np.arange(seq)
    rowcol = np.stack([(pos // GRID_W).astype(f32), (pos % GRID_W).astype(f32)], axis=1)
    sec = ATTN_HEAD_DIM // 2
    inv_freq = (f32(ROPE_THETA) ** (-np.arange(0, sec, 2, dtype=f32) / f32(sec))).astype(f32)
    d = np.arange(LANES) % ATTN_HEAD_DIM
    axis = d // sec
    idx = d % (sec // 2)
    sign = np.where((d % sec) < sec // 2, -1.0, 1.0).astype(f32)
    ang = (rowcol[:, axis] * inv_freq[idx][None, :]).astype(f32)
    return jnp.asarray(np.cos(ang).astype(f32)), jnp.asarray((np.sin(ang) * sign[None, :]).astype(f32))


def kernel(x, p, norm_mix, w_in, conv_w, q_norm, k_norm, a_log, dt_bias, dn_norm, w_out,
           norm_ffn, w_gate, w_up, w_down, norm_ple, w_ple_gate, w_ple, norm_final):
    b, s, d = x.shape
    depth = w_in.shape[0]
    t = b * s
    tm, tq = min(ROW_TILE, s), min(Q_TILE, s)
    cos_t, sin_t = _rope_tables(s)

    h = x.reshape(t, d)
    for i in range(depth):
        wi = w_in[i].astype(BF16)
        w_cat = jnp.pad(wi, ((0, 0), (0, LANES - 4 * DN_HEADS)))

        qg = jnp.tile(q_norm[i], LANES // ATTN_HEAD_DIM)[None, :]
        kg = jnp.tile(k_norm[i], LANES // ATTN_HEAD_DIM)[None, :]
        q, k, va, vb, dqkv, dz, ba = _inproj(h, norm_mix[i][None, :], w_cat, cos_t, sin_t, qg, kg, s, tm)

        score_bound = (ATTN_HEAD_DIM ** 0.5) * jnp.max(jnp.abs(q_norm[i])) * jnp.max(jnp.abs(k_norm[i]))
        bounded = (score_bound <= MAX_UNSHIFTED_SCORE).astype(jnp.int32).reshape(1)
        attn = _attention(bounded, q.reshape(b, s, ATTN_Q_W), k.reshape(b, s // tm, tm, KV_W),
                          va.reshape(b, s // tm, V_ROWS, tm), vb.reshape(b, s // tm, V_ROWS, tm), tq)

        pad = (2 * DN_HEADS, LANES - 4 * DN_HEADS)
        alog_pad = jnp.pad(a_log[i], pad)[None, :]
        dtb_pad = jnp.pad(dt_bias[i], pad)[None, :]
        dqkv3 = dqkv.reshape(b, s, DN_QKV_W)
        ba3 = ba.reshape(b, s, LANES)
        o_f, o_b = _dn_scan(*_dn_local(dqkv3, ba3, conv_w[i], alog_pad, dtb_pad, jnp.asarray(_dn_masks())))

        h = _post(h, attn.reshape(t, ATTN_Q_W), o_f.reshape(t, DN_V_W), o_b.reshape(t, DN_V_W), dz,
                  p[i].reshape(t, -1), dn_norm[i][None, :], w_out[i].astype(BF16),
                  norm_ffn[i][None, :], w_gate[i].astype(BF16), w_up[i].astype(BF16),
                  w_down[i].astype(BF16), norm_ple[i][None, :], w_ple_gate[i].astype(BF16),
                  w_ple[i].astype(BF16),
                  norm_final[None, :], tm, FF_CHUNK, i == depth - 1)
    return h.reshape(b, s, d)
```

```python
import functools

import numpy as np
import jax
import jax.numpy as jnp
from jax import lax
from jax.experimental import pallas as pl
from jax.experimental.pallas import tpu as pltpu

F32 = jnp.float32
BF16 = jnp.bfloat16

LANES = 128
SUBLANES = 8
VMEM_LIMIT = 60000 * 1024

GRID_W = 64
ATTN_HEADS = 8
ATTN_KV_HEADS = 2
ATTN_HEAD_DIM = 64
ATTN_GROUP = ATTN_HEADS // ATTN_KV_HEADS
ROPE_THETA = 10000.0
DN_HEADS = 4
DN_HEAD = 128
DN_CHUNK = 128
DN_DIRS = 2
DN_LEVELS = 7
N_MASKS = 2 + DN_LEVELS
CONV_W = 5
EPS = 1e-6
LOG2E = float(np.log2(np.e))

KV_W = ATTN_KV_HEADS * ATTN_HEAD_DIM
DN_QKV_W = 3 * DN_HEADS * DN_HEAD
DN_V_W = DN_HEADS * DN_HEAD
ATTN_Q_W = ATTN_HEADS * ATTN_HEAD_DIM

ROW_TILE = 512
Q_TILE = 512
KV_UNROLL = 8
V_ROWS = ATTN_HEAD_DIM + 16
Q_SCALE = ATTN_HEAD_DIM ** -0.5 * LOG2E
MAX_UNSHIFTED_SCORE = 40.0
FF_CHUNK = 256
DN_BLOCK = 256
DN_LOCAL_BLOCK = 1024
DN_SUB = LANES
CONV_HALO = SUBLANES


def _dot(a, b):
    return jnp.dot(a, b, preferred_element_type=F32)


def _dot_nt(a, b):
    return lax.dot_general(a, b, (((1,), (1,)), ((), ())), preferred_element_type=F32)


def _rms(x, gain):
    return x * lax.rsqrt(jnp.mean(x * x, axis=-1, keepdims=True) + EPS) * gain


def _sigmoid(x):
    return 1.0 / (1.0 + jnp.exp(-x))


def _inproj_kernel(x_ref, gain_ref, w_ref, cos_ref, sin_ref, qg_ref, kg_ref,
                   q_ref, k_ref, va_ref, vb_ref, dqkv_ref, dz_ref, ba_ref):
    tm = x_ref.shape[0]
    hb = _rms(x_ref[...], gain_ref[...]).astype(BF16)
    cos = cos_ref[...]
    sin = sin_ref[...]
    lane = lax.broadcasted_iota(jnp.int32, (tm, LANES), 1)
    low16 = (lane % 32) < 16
    head0 = lane < ATTN_HEAD_DIM

    def norm_rope(t, gain):
        t2 = t * t
        ss0 = jnp.sum(jnp.where(head0, t2, 0.0), axis=-1, keepdims=True) * (1.0 / ATTN_HEAD_DIM)
        ss1 = jnp.sum(jnp.where(head0, 0.0, t2), axis=-1, keepdims=True) * (1.0 / ATTN_HEAD_DIM)
        tn = t * jnp.where(head0, lax.rsqrt(ss0 + EPS), lax.rsqrt(ss1 + EPS)) * gain
        partner = jnp.where(low16, pltpu.roll(tn, LANES - 16, 1), pltpu.roll(tn, 16, 1))
        return tn * cos + partner * sin

    c0 = 0
    q_all = _dot(hb, w_ref[:, c0:c0 + ATTN_Q_W])
    for c in range(ATTN_Q_W // LANES):
        sl = slice(c * LANES, (c + 1) * LANES)
        q_ref[:, sl] = (norm_rope(q_all[:, sl], qg_ref[...]) * Q_SCALE).astype(BF16)
    c0 += ATTN_Q_W

    kv = _dot(hb, w_ref[:, c0:c0 + 2 * KV_W])
    k_ref[...] = norm_rope(kv[:, :KV_W], kg_ref[...]).astype(BF16)
    v_t = jnp.transpose(kv[:, KV_W:])
    ones = jnp.ones((V_ROWS - ATTN_HEAD_DIM, tm), F32)
    va_ref[0] = jnp.concatenate([v_t[:ATTN_HEAD_DIM], ones], axis=0).astype(BF16)
    vb_ref[0] = jnp.concatenate([v_t[ATTN_HEAD_DIM:], ones], axis=0).astype(BF16)
    c0 += 2 * KV_W

    dqkv_ref[...] = _dot(hb, w_ref[:, c0:c0 + DN_QKV_W]).astype(BF16)
    c0 += DN_QKV_W
    zb = _dot(hb, w_ref[:, c0:c0 + DN_V_W + LANES])
    dz_ref[...] = zb[:, :DN_V_W].astype(BF16)
    ba_ref[...] = zb[:, DN_V_W:]


def _inproj(xf, gain, w_cat, cos_t, sin_t, qg, kg, seq, tm):
    t, d = xf.shape
    nseq = seq // tm
    wcols = w_cat.shape[1]
    row = lambda i: (i, 0)
    const = lambda i: (0, 0)
    tab = lambda i: (i % nseq, 0)
    return pl.pallas_call(
        _inproj_kernel,
        grid=(t // tm,),
        in_specs=[
            pl.BlockSpec((tm, d), row),
            pl.BlockSpec((1, d), const),
            pl.BlockSpec((d, wcols), const),
            pl.BlockSpec((tm, LANES), tab),
            pl.BlockSpec((tm, LANES), tab),
            pl.BlockSpec((1, LANES), const),
            pl.BlockSpec((1, LANES), const),
        ],
        out_specs=[
            pl.BlockSpec((tm, ATTN_Q_W), row),
            pl.BlockSpec((tm, KV_W), row),
            pl.BlockSpec((1, V_ROWS, tm), lambda i: (i, 0, 0)),
            pl.BlockSpec((1, V_ROWS, tm), lambda i: (i, 0, 0)),
            pl.BlockSpec((tm, DN_QKV_W), row),
            pl.BlockSpec((tm, DN_V_W), row),
            pl.BlockSpec((tm, LANES), row),
        ],
        out_shape=[
            jax.ShapeDtypeStruct((t, ATTN_Q_W), BF16),
            jax.ShapeDtypeStruct((t, KV_W), BF16),
            jax.ShapeDtypeStruct((t // tm, V_ROWS, tm), BF16),
            jax.ShapeDtypeStruct((t // tm, V_ROWS, tm), BF16),
            jax.ShapeDtypeStruct((t, DN_QKV_W), BF16),
            jax.ShapeDtypeStruct((t, DN_V_W), BF16),
            jax.ShapeDtypeStruct((t, LANES), F32),
        ],
        compiler_params=pltpu.CompilerParams(
            dimension_semantics=("parallel",), vmem_limit_bytes=VMEM_LIMIT),
        name="inproj",
    )(xf, gain, w_cat, cos_t, sin_t, qg, kg)


def _attn_kernel(bounded_ref, q_ref, k_ref, vat_ref, vbt_ref, o_ref, qt_ref, acc_ref, m_ref):
    tq = q_ref.shape[1]
    nkv = k_ref.shape[1]
    unroll = KV_UNROLL if nkv % KV_UNROLL == 0 else 1
    zeros = jnp.zeros((ATTN_HEAD_DIM, tq), BF16)
    for c in range(ATTN_Q_W // LANES):
        pair_t = jnp.transpose(q_ref[0, :, c * LANES:(c + 1) * LANES].astype(F32)).astype(BF16)
        for e in range(2):
            h = 2 * c + e
            g, j = divmod(h, ATTN_GROUP)
            q_t = pair_t[e * ATTN_HEAD_DIM:(e + 1) * ATTN_HEAD_DIM]
            qt_ref[g, :, j * tq:(j + 1) * tq] = jnp.concatenate(
                [q_t, zeros] if g == 0 else [zeros, q_t], axis=0)
    acc_ref[...] = jnp.zeros(acc_ref.shape, F32)
    vt_refs = (vat_ref, vbt_ref)

    @pl.when(bounded_ref[0] == 1)
    def _():
        def body(i, carry):
            for g in range(ATTN_KV_HEADS):
                upd = None
                for t in range(unroll):
                    tile = i * unroll + t
                    p_t = jnp.exp2(_dot(k_ref[0, tile], qt_ref[g])).astype(BF16)
                    part = _dot(vt_refs[g][0, tile], p_t)
                    upd = part if upd is None else upd + part
                acc_ref[g] += upd
            return carry

        lax.fori_loop(0, nkv // unroll, body, 0)

    @pl.when(bounded_ref[0] == 0)
    def _():
        m_ref[...] = jnp.full(m_ref.shape, -jnp.inf, F32)

        def body(i, carry):
            kt = k_ref[0, i]
            for g in range(ATTN_KV_HEADS):
                s_t = _dot(kt, qt_ref[g])
                m_prev = m_ref[g]
                m_new = jnp.maximum(m_prev, jnp.max(s_t, axis=0, keepdims=True))
                p_t = jnp.exp2(s_t - m_new[0:1, :]).astype(BF16)
                alpha = jnp.exp2(m_prev - m_new)[0:1, :]
                acc_ref[g] = alpha * acc_ref[g] + _dot(vt_refs[g][0, i], p_t)
                m_ref[g] = m_new
            return carry

        lax.fori_loop(0, nkv, body, 0)

    for g in range(ATTN_KV_HEADS):
        a = acc_ref[g]
        o_t = a[:ATTN_HEAD_DIM] / a[ATTN_HEAD_DIM:ATTN_HEAD_DIM + 1]
        for jp in range(ATTN_GROUP // 2):
            pair = jnp.concatenate(
                [o_t[:, (2 * jp) * tq:(2 * jp + 1) * tq], o_t[:, (2 * jp + 1) * tq:(2 * jp + 2) * tq]],
                axis=0)
            c = g * (ATTN_GROUP // 2) + jp
            o_ref[0, :, c * LANES:(c + 1) * LANES] = jnp.transpose(pair).astype(BF16)


def _attention(bounded, q, k, vat, vbt, tq):
    b, s, _ = q.shape
    _, nkv, tk, _ = k.shape
    cols = ATTN_GROUP * tq
    return pl.pallas_call(
        _attn_kernel,
        grid_spec=pltpu.PrefetchScalarGridSpec(
            num_scalar_prefetch=1,
            grid=(b, s // tq),
            in_specs=[
                pl.BlockSpec((1, tq, ATTN_Q_W), lambda bi, qi, flag: (bi, qi, 0)),
                pl.BlockSpec((1, nkv, tk, KV_W), lambda bi, qi, flag: (bi, 0, 0, 0)),
                pl.BlockSpec((1, nkv, V_ROWS, tk), lambda bi, qi, flag: (bi, 0, 0, 0)),
                pl.BlockSpec((1, nkv, V_ROWS, tk), lambda bi, qi, flag: (bi, 0, 0, 0)),
            ],
            out_specs=pl.BlockSpec((1, tq, ATTN_Q_W), lambda bi, qi, flag: (bi, qi, 0)),
            scratch_shapes=[
                pltpu.VMEM((ATTN_KV_HEADS, LANES, cols), BF16),
                pltpu.VMEM((ATTN_KV_HEADS, V_ROWS, cols), F32),
                pltpu.VMEM((ATTN_KV_HEADS, SUBLANES, cols), F32),
            ],
        ),
        out_shape=jax.ShapeDtypeStruct((b, s, ATTN_Q_W), BF16),
        compiler_params=pltpu.CompilerParams(
            dimension_semantics=("parallel", "arbitrary"), vmem_limit_bytes=VMEM_LIMIT),
        name="attention",
    )(bounded, q, k, vat, vbt)


def _dn_masks():
    n = DN_SUB
    row = np.arange(n)[:, None]
    col = np.arange(n)[None, :]
    same = (row // DN_CHUNK) == (col // DN_CHUNK)
    out = []
    for rev in (False, True):
        a, b = (col, row) if rev else (row, col)
        out.append(same & (a >= b))
        out.append(same & (a > b))
        for lb in range(DN_LEVELS):
            out.append(((a >> lb) == (b >> lb) + 1) & (((a >> lb) & 1) == 1))
    return np.stack(out).astype(np.float32)


def _dn_local_kernel(qkv_ref, prev_ref, next_ref, ba_ref, convw_ref, alog_ref, dtb_ref, mask_ref,
                     wq_ref, u_ref, ot_ref, kdt_ref, gt_ref):
    n = qkv_ref.shape[1]
    nch = n // DN_CHUNK
    blk = pl.program_id(1)
    nblk = pl.num_programs(1)

    rows = n + 2 * CONV_HALO
    cols = []
    for c in range(DN_QKV_W // LANES):
        lanes = slice(c * LANES, (c + 1) * LANES)
        top = jnp.where(blk == 0, 0.0, prev_ref[0, :, lanes].astype(F32))
        bot = jnp.where(blk == nblk - 1, 0.0, next_ref[0, :, lanes].astype(F32))
        ext = jnp.concatenate([top, qkv_ref[0, :, lanes].astype(F32), bot], axis=0)
        acc = None
        for j in range(CONV_W):
            shift = (CONV_W // 2 - j) % rows
            xs = ext if shift == 0 else pltpu.roll(ext, shift, 0)
            term = xs[CONV_HALO:CONV_HALO + n] * convw_ref[j:j + 1, lanes]
            acc = term if acc is None else acc + term
        cols.append(acc * _sigmoid(acc))
    y = jnp.concatenate(cols, axis=1)

    ba = ba_ref[0]
    beta_all = _sigmoid(ba)
    sp_arg = ba + dtb_ref[...]
    softplus = jnp.maximum(sp_arg, 0.0) + jnp.log(1.0 + jnp.exp(-jnp.abs(sp_arg)))
    g2 = (-LOG2E) * jnp.exp(alog_ref[...]) * softplus

    in_chunk = lax.broadcasted_iota(jnp.int32, (n, LANES), 0) % DN_CHUNK
    g_fwd = g2
    shift = 1
    while shift < DN_CHUNK:
        g_fwd = g_fwd + jnp.where(in_chunk >= shift, pltpu.roll(g_fwd, shift, 0), 0.0)
        shift *= 2
    total = jnp.concatenate(
        [jnp.broadcast_to(g_fwd[(c + 1) * DN_CHUNK - 1:(c + 1) * DN_CHUNK, :], (DN_CHUNK, LANES))
         for c in range(nch)], axis=0)
    g_bwd = total - g_fwd + g2
    g_cols = (g_fwd, g_bwd)
    g_rows = (jnp.transpose(g_fwd), jnp.transpose(g_bwd))

    m = DN_SUB
    row = lax.broadcasted_iota(jnp.int32, (m, m), 0)
    col = lax.broadcasted_iota(jnp.int32, (m, m), 1)
    eye = jnp.where(row == col, 1.0, 0.0)
    subs = [slice(s * m, (s + 1) * m) for s in range(n // m)]

    qs, ks, vs, kk, qk_raw = [], [], [], [], []
    for h in range(DN_HEADS):
        q = y[:, h * DN_HEAD:(h + 1) * DN_HEAD]
        k = y[:, (DN_HEADS + h) * DN_HEAD:(DN_HEADS + h + 1) * DN_HEAD]
        q = q * lax.rsqrt(jnp.sum(q * q, axis=-1, keepdims=True) + EPS) * (DN_HEAD ** -0.5)
        k = k * lax.rsqrt(jnp.sum(k * k, axis=-1, keepdims=True) + EPS)
        k_bf = k.astype(BF16)
        q_bf = q.astype(BF16)
        qs.append(q)
        ks.append(k)
        vs.append(y[:, (2 * DN_HEADS + h) * DN_HEAD:(2 * DN_HEADS + h + 1) * DN_HEAD])
        kk.append([_dot_nt(k_bf[r], k_bf[r]) for r in subs])
        qk_raw.append([_dot_nt(q_bf[r], k_bf[r]) for r in subs])

    hds = [(d, h) for d in range(DN_DIRS) for h in range(DN_HEADS)]
    prob = [(i, s) for i in range(len(hds)) for s in range(len(subs))]
    a_bf, x_inv, qk_bf, rhs_bf, qd, g_tot = {}, {}, {}, [], [], []
    for i, (d, h) in enumerate(hds):
        lane = 2 * DN_HEADS + i
        gc = jnp.broadcast_to(g_cols[d][:, lane:lane + 1], (n, LANES))
        beta = jnp.broadcast_to(beta_all[:, i:i + 1], (n, LANES))
        for s, r in enumerate(subs):
            g_row = g_rows[d][lane:lane + 1, r]
            decay = jnp.exp2(jnp.minimum(gc[r] - g_row, 0.0))
            a_mat = kk[h][s] * beta[r] * (decay * mask_ref[d * N_MASKS + 1])
            a_bf[i, s] = a_mat.astype(BF16)
            x_inv[i, s] = eye - a_mat * mask_ref[d * N_MASKS + 2]
            qk_bf[i, s] = (qk_raw[h][s] * (decay * mask_ref[d * N_MASKS])).astype(BF16)
        e_gc = jnp.exp2(gc)
        rhs_bf.append(jnp.concatenate([vs[h] * beta, ks[h] * beta * e_gc], axis=1).astype(BF16))
        qd.append(qs[h] * e_gc)

        edge = 0 if d else DN_CHUNK - 1
        gl_rows = [gc[c * DN_CHUNK + edge:c * DN_CHUNK + edge + 1, :] for c in range(nch)]
        gl = jnp.concatenate([jnp.broadcast_to(t, (DN_CHUNK, LANES)) for t in gl_rows], axis=0)
        g_tot.append([jnp.exp2(t) for t in gl_rows])
        kdt_ref[0, i] = jnp.transpose(ks[h] * jnp.exp2(gl - gc)).astype(BF16)

    for lb in range(1, DN_LEVELS):
        for i, s in prob:
            x_bf = x_inv[i, s].astype(BF16)
            upd = _dot(_dot(x_bf, a_bf[i, s]).astype(BF16), x_bf)
            x_inv[i, s] = x_inv[i, s] - upd * mask_ref[hds[i][0] * N_MASKS + 2 + lb]

    sols = {(i, s): _dot(x_inv[i, s].astype(BF16), rhs_bf[i][subs[s]]) for i, s in prob}
    corrs = {(i, s): _dot(qk_bf[i, s], sols[i, s].astype(BF16)) for i, s in prob}
    for i in range(len(hds)):
        sol = jnp.concatenate([sols[i, s] for s in range(len(subs))], axis=0)
        corr = jnp.concatenate([corrs[i, s] for s in range(len(subs))], axis=0)
        w_bf = sol[:, DN_HEAD:].astype(BF16)
        qe_bf = (qd[i] - corr[:, DN_HEAD:]).astype(BF16)
        pieces = []
        for c in range(nch):
            sl = slice(c * DN_CHUNK, (c + 1) * DN_CHUNK)
            pieces += [w_bf[sl], qe_bf[sl]]
        wq_ref[0, i] = jnp.concatenate(pieces, axis=0)
        u_ref[0, i] = sol[:, :DN_HEAD].astype(BF16)
        ot_ref[0, i] = corr[:, :DN_HEAD].astype(BF16)

    for c in range(nch):
        gt_ref[0, c] = jnp.concatenate([t[c] for t in g_tot], axis=0)


def _dn_local(dqkv, ba, conv_w, alog_pad, dtb_pad, masks):
    b, s, _ = dqkv.shape
    n = min(DN_LOCAL_BLOCK, s)
    nblk = s // n
    per = n // CONV_HALO
    nhalo = s // CONV_HALO
    hds = DN_DIRS * DN_HEADS
    return pl.pallas_call(
        _dn_local_kernel,
        grid=(b, nblk),
        in_specs=[
            pl.BlockSpec((1, n, DN_QKV_W), lambda bi, c: (bi, c, 0)),
            pl.BlockSpec((1, CONV_HALO, DN_QKV_W), lambda bi, c: (bi, jnp.maximum(c * per - 1, 0), 0)),
            pl.BlockSpec((1, CONV_HALO, DN_QKV_W),
                         lambda bi, c: (bi, jnp.minimum((c + 1) * per, nhalo - 1), 0)),
            pl.BlockSpec((1, n, LANES), lambda bi, c: (bi, c, 0)),
            pl.BlockSpec((CONV_W, DN_QKV_W), lambda bi, c: (0, 0)),
            pl.BlockSpec((1, LANES), lambda bi, c: (0, 0)),
            pl.BlockSpec((1, LANES), lambda bi, c: (0, 0)),
            pl.BlockSpec(masks.shape, lambda bi, c: (0, 0, 0)),
        ],
        out_specs=[
            pl.BlockSpec((1, hds, 2 * n, DN_HEAD), lambda bi, c: (bi, 0, c, 0)),
            pl.BlockSpec((1, hds, n, DN_HEAD), lambda bi, c: (bi, 0, c, 0)),
            pl.BlockSpec((1, hds, n, DN_HEAD), lambda bi, c: (bi, 0, c, 0)),
            pl.BlockSpec((1, hds, DN_HEAD, n), lambda bi, c: (bi, 0, 0, c)),
            pl.BlockSpec((1, n // DN_CHUNK, hds, LANES), lambda bi, c: (bi, c, 0, 0)),
        ],
        out_shape=[
            jax.ShapeDtypeStruct((b, hds, 2 * s, DN_HEAD), BF16),
            jax.ShapeDtypeStruct((b, hds, s, DN_HEAD), BF16),
            jax.ShapeDtypeStruct((b, hds, s, DN_HEAD), BF16),
            jax.ShapeDtypeStruct((b, hds, DN_HEAD, s), BF16),
            jax.ShapeDtypeStruct((b, s // DN_CHUNK, hds, LANES), F32),
        ],
        compiler_params=pltpu.CompilerParams(
            dimension_semantics=("parallel", "parallel"), vmem_limit_bytes=VMEM_LIMIT),
        name="deltanet_local",
    )(dqkv, dqkv, dqkv, ba, conv_w, alog_pad, dtb_pad, masks)


def _dn_scan_kernel(wqf_ref, wqb_ref, uf_ref, ub_ref, otf_ref, otb_ref, kdf_ref, kdb_ref,
                    gtf_ref, gtb_ref, of_ref, ob_ref, s_ref):
    n = DN_BLOCK
    nch = n // DN_CHUNK

    @pl.when(pl.program_id(1) == 0)
    def _():
        s_ref[...] = jnp.zeros(s_ref.shape, F32)

    refs = ((wqf_ref, uf_ref, otf_ref, kdf_ref, gtf_ref), (wqb_ref, ub_ref, otb_ref, kdb_ref, gtb_ref))
    hds = DN_DIRS * DN_HEADS
    states = [s_ref[hd] for hd in range(hds)]
    outs = [[None] * nch for _ in range(hds)]
    for r in range(nch):
        for d in range(DN_DIRS):
            wq_ref, u_ref, ot_ref, kd_ref, gt_ref = refs[d]
            c = nch - 1 - r if d else r
            rows = slice(c * DN_CHUNK, (c + 1) * DN_CHUNK)
            for h in range(DN_HEADS):
                hd = d * DN_HEADS + h
                s_bf = states[hd].astype(BF16)
                res = _dot(wq_ref[0, h, 2 * c * DN_CHUNK:2 * (c + 1) * DN_CHUNK, :], s_bf)
                vn = u_ref[0, h, rows, :].astype(F32) - res[:DN_CHUNK]
                outs[hd][c] = res[DN_CHUNK:] + ot_ref[0, h, rows, :].astype(F32)
                pieces = []
                if c:
                    pieces.append(jnp.zeros((c * DN_CHUNK, LANES), BF16))
                pieces.append(vn.astype(BF16))
                if nch - 1 - c:
                    pieces.append(jnp.zeros(((nch - 1 - c) * DN_CHUNK, LANES), BF16))
                vn_pad = jnp.concatenate(pieces, axis=0)
                states[hd] = states[hd] * gt_ref[0, c, hd:hd + 1, :] + _dot(kd_ref[0, h], vn_pad)
    for hd in range(hds):
        s_ref[hd] = states[hd]
    for d, o_ref in enumerate((of_ref, ob_ref)):
        for h in range(DN_HEADS):
            o_ref[0, :, h * DN_HEAD:(h + 1) * DN_HEAD] = jnp.concatenate(
                outs[d * DN_HEADS + h], axis=0).astype(BF16)


def _dn_scan(wq, u, ot, kdt, gt):
    b, hds, s, _ = u.shape
    n = DN_BLOCK
    nblk = s // n
    nch = n // DN_CHUNK
    fwd = lambda bi, c: c
    bwd = lambda bi, c: nblk - 1 - c

    def pair(shape, index):
        return [pl.BlockSpec(shape, functools.partial(index, 0, fwd)),
                pl.BlockSpec(shape, functools.partial(index, 1, bwd))]

    rows = lambda d, pos, bi, c: (bi, d, pos(bi, c), 0)
    lanes = lambda d, pos, bi, c: (bi, d, 0, pos(bi, c))
    gts = lambda d, pos, bi, c: (bi, pos(bi, c), 0, 0)
    return pl.pallas_call(
        _dn_scan_kernel,
        grid=(b, nblk),
        in_specs=(pair((1, DN_HEADS, 2 * n, DN_HEAD), rows) + pair((1, DN_HEADS, n, DN_HEAD), rows)
                  + pair((1, DN_HEADS, n, DN_HEAD), rows) + pair((1, DN_HEADS, DN_HEAD, n), lanes)
                  + pair((1, nch, hds, LANES), gts)),
        out_specs=[
            pl.BlockSpec((1, n, DN_V_W), lambda bi, c: (bi, c, 0)),
            pl.BlockSpec((1, n, DN_V_W), lambda bi, c: (bi, nblk - 1 - c, 0)),
        ],
        out_shape=[jax.ShapeDtypeStruct((b, s, DN_V_W), BF16)] * 2,
        scratch_shapes=[pltpu.VMEM((hds, DN_HEAD, DN_HEAD), F32)],
        compiler_params=pltpu.CompilerParams(
            dimension_semantics=("parallel", "arbitrary"), vmem_limit_bytes=VMEM_LIMIT),
        name="deltanet_scan",
    )(wq, wq, u, u, ot, ot, kdt, kdt, gt, gt)


def _post_kernel(ff_chunk, final_norm, x_ref, attn_ref, of_ref, ob_ref, dz_ref, p_ref, dng_ref, wout_ref,
                 nffn_ref, wg_ref, wu_ref, wd_ref, nple_ref, wpg_ref, wple_ref, nfin_ref, out_ref):
    o = of_ref[...].astype(F32) + ob_ref[...].astype(F32)
    z = dz_ref[...].astype(F32)
    parts = [attn_ref[...]]
    for h in range(DN_HEADS):
        sl = slice(h * DN_HEAD, (h + 1) * DN_HEAD)
        zh = z[:, sl]
        parts.append((_rms(o[:, sl], dng_ref[...]) * (zh * _sigmoid(zh))).astype(BF16))
    mixed = jnp.concatenate(parts, axis=1)
    h1 = x_ref[...] + _dot(mixed, wout_ref[...])

    hn = _rms(h1, nffn_ref[...]).astype(BF16)
    h2 = h1
    ff = wg_ref.shape[1]
    for c in range(ff // ff_chunk):
        sl = slice(c * ff_chunk, (c + 1) * ff_chunk)
        gate = _dot(hn, wg_ref[:, sl])
        up = _dot(hn, wu_ref[:, sl])
        act = (gate * _sigmoid(gate) * up).astype(BF16)
        h2 = h2 + _dot(act, wd_ref[sl, :])

    hp = _rms(h2, nple_ref[...]).astype(BF16)
    gate = _sigmoid(_dot(hp, wpg_ref[...]))
    h3 = h2 + gate * _dot(p_ref[...].astype(BF16), wple_ref[...])
    out_ref[...] = _rms(h3, nfin_ref[...]) if final_norm else h3


def _post(xf, attn, o_f, o_b, dz, pf, dng, w_out, nffn, w_gate, w_up, w_down, nple, w_pg, w_ple,
          nfin, tm, ff_chunk, final_norm):
    t, d = xf.shape
    row = lambda i: (i, 0)
    const = lambda i: (0, 0)

    def resident(shape):
        return pl.BlockSpec(shape, const, pipeline_mode=pl.Buffered(1))

    ff = w_gate.shape[1]
    return pl.pallas_call(
        functools.partial(_post_kernel, ff_chunk, final_norm),
        grid=(t // tm,),
        in_specs=[
            pl.BlockSpec((tm, d), row),
            pl.BlockSpec((tm, ATTN_Q_W), row),
            pl.BlockSpec((tm, DN_V_W), row),
            pl.BlockSpec((tm, DN_V_W), row),
            pl.BlockSpec((tm, DN_V_W), row),
            pl.BlockSpec((tm, pf.shape[1]), row),
            resident((1, DN_HEAD)),
            resident((ATTN_Q_W + DN_V_W, d)),
            resident((1, d)),
            resident((d, ff)),
            resident((d, ff)),
            resident((ff, d)),
            resident((1, d)),
            resident((d, d)),
            resident((pf.shape[1], d)),
            resident((1, d)),
        ],
        out_specs=pl.BlockSpec((tm, d), row),
        out_shape=jax.ShapeDtypeStruct((t, d), F32),
        compiler_params=pltpu.CompilerParams(
            dimension_semantics=("parallel",), vmem_limit_bytes=VMEM_LIMIT),
        name="post",
    )(xf, attn, o_f, o_b, dz, pf, dng, w_out, nffn, w_gate, w_up, w_down, nple, w_pg, w_ple, nfin)


def _rope_tables(seq):
    f32 = np.float32
    pos = np.arange(seq)
    rowcol = np.stack([(pos // GRID_W).astype(f32), (pos % GRID_W).astype(f32)], axis=1)
    sec = ATTN_HEAD_DIM // 2
    inv_freq = (f32(ROPE_THETA) ** (-np.arange(0, sec, 2, dtype=f32) / f32(sec))).astype(f32)
    d = np.arange(LANES) % ATTN_HEAD_DIM
    axis = d // sec
    idx = d % (sec // 2)
    sign = np.where((d % sec) < sec // 2, -1.0, 1.0).astype(f32)
    ang = (rowcol[:, axis] * inv_freq[idx][None, :]).astype(f32)
    return jnp.asarray(np.cos(ang).astype(f32)), jnp.asarray((np.sin(ang) * sign[None, :]).astype(f32))


def kernel(x, p, norm_mix, w_in, conv_w, q_norm, k_norm, a_log, dt_bias, dn_norm, w_out,
           norm_ffn, w_gate, w_up, w_down, norm_ple, w_ple_gate, w_ple, norm_final):
    b, s, d = x.shape
    depth = w_in.shape[0]
    t = b * s
    tm, tq = min(ROW_TILE, s), min(Q_TILE, s)
    cos_t, sin_t = _rope_tables(s)

    h = x.reshape(t, d)
    for i in range(depth):
        wi = w_in[i].astype(BF16)
        w_cat = jnp.pad(wi, ((0, 0), (0, LANES - 4 * DN_HEADS)))

        qg = jnp.tile(q_norm[i], LANES // ATTN_HEAD_DIM)[None, :]
        kg = jnp.tile(k_norm[i], LANES // ATTN_HEAD_DIM)[None, :]
        q, k, va, vb, dqkv, dz, ba = _inproj(h, norm_mix[i][None, :], w_cat, cos_t, sin_t, qg, kg, s, tm)

        score_bound = (ATTN_HEAD_DIM ** 0.5) * jnp.max(jnp.abs(q_norm[i])) * jnp.max(jnp.abs(k_norm[i]))
        bounded = (score_bound <= MAX_UNSHIFTED_SCORE).astype(jnp.int32).reshape(1)
        attn = _attention(bounded, q.reshape(b, s, ATTN_Q_W), k.reshape(b, s // tm, tm, KV_W),
                          va.reshape(b, s // tm, V_ROWS, tm), vb.reshape(b, s // tm, V_ROWS, tm), tq)

        pad = (2 * DN_HEADS, LANES - 4 * DN_HEADS)
        alog_pad = jnp.pad(a_log[i], pad)[None, :]
        dtb_pad = jnp.pad(dt_bias[i], pad)[None, :]
        dqkv3 = dqkv.reshape(b, s, DN_QKV_W)
        ba3 = ba.reshape(b, s, LANES)
        o_f, o_b = _dn_scan(*_dn_local(dqkv3, ba3, conv_w[i], alog_pad, dtb_pad, jnp.asarray(_dn_masks())))

        h = _post(h, attn.reshape(t, ATTN_Q_W), o_f.reshape(t, DN_V_W), o_b.reshape(t, DN_V_W), dz,
                  p[i].reshape(t, -1), dn_norm[i][None, :], w_out[i].astype(BF16),
                  norm_ffn[i][None, :], w_gate[i].astype(BF16), w_up[i].astype(BF16),
                  w_down[i].astype(BF16), norm_ple[i][None, :], w_ple_gate[i].astype(BF16),
                  w_ple[i].astype(BF16),
                  norm_final[None, :], tm, FF_CHUNK, i == depth - 1)
    return h.reshape(b, s, d)
```
